```python
import math
import jax, jax.numpy as jnp
from jax import lax
import numpy as np

D_MODEL = 1024
BATCH = 32
SEQ = 2048
DEPTH = 2

EPS = 1e-6
RET_HEADS = 4
RET_QK_DIM = 128
RET_V_DIM = 256
RET_CHUNK = 128
RET_ROPE_BASE = 10000.0
SSM_HEADS = 16
SSM_HEAD_DIM = 64
SSM_GROUPS = 4
SSM_STATE = 128
SSM_CONV = 4
SSM_CHUNK = 128
SSM_INNER = SSM_HEADS * SSM_HEAD_DIM
SSM_XBC = SSM_INNER + 2 * SSM_GROUPS * SSM_STATE
ATT_PATTERNS = ((128, 1), (512, 4), (2048, 16))
ATT_SLOTS = 4
ATT_HEAD_DIM = 64
ATT_HEADS = len(ATT_PATTERNS) * ATT_SLOTS
N_BRANCHES = 3
D_FF = -(-8 * D_MODEL // (3 * 256)) * 256

IN_WIDTHS = (
    RET_HEADS * RET_QK_DIM,
    RET_HEADS * RET_QK_DIM,
    RET_HEADS * RET_V_DIM,
    RET_HEADS * RET_V_DIM,
    SSM_INNER,
    SSM_XBC,
    SSM_HEADS,
    ATT_HEADS * ATT_HEAD_DIM,
    ATT_HEADS * ATT_HEAD_DIM,
    ATT_HEADS * ATT_HEAD_DIM,
    N_BRANCHES * D_MODEL,
)
IN_WIDTH = sum(IN_WIDTHS)
IN_SPLITS = tuple(int(v) for v in np.cumsum(IN_WIDTHS)[:-1])

kernel_name = "hybrid_retention_ssd_dilated_attn_block"


def rms_norm(x, w):
    xf = x.astype(jnp.float32)
    y = xf * lax.rsqrt(jnp.mean(xf * xf, axis=-1, keepdims=True) + EPS)
    return (y * w.astype(jnp.float32)).astype(x.dtype)


def to_chunks(t, chunk):
    b, s = t.shape[:2]
    return jnp.moveaxis(t.reshape((b, s // chunk, chunk) + t.shape[2:]), 1, 0)


def from_chunks(t):
    t = jnp.moveaxis(t, 0, 1)
    return t.reshape((t.shape[0], t.shape[1] * t.shape[2]) + t.shape[3:])


def rotary(t, pos):
    half = t.shape[-1] // 2
    inv_freq = RET_ROPE_BASE ** (-jnp.arange(half, dtype=jnp.float32) / half)
    ang = pos[:, None] * inv_freq[None, :]
    cos, sin = jnp.cos(ang)[:, None, :], jnp.sin(ang)[:, None, :]
    t1, t2 = t[..., :half], t[..., half:]
    return jnp.concatenate([t1 * cos - t2 * sin, t1 * sin + t2 * cos], axis=-1)


def retention(q, k, v, g):
    dtype = v.dtype
    q, k, v, g = (t.astype(jnp.float32) for t in (q, k, v, g))
    b, s, h, dk = q.shape
    dv = v.shape[-1]
    pos = jnp.arange(s, dtype=jnp.float32)
    q = rotary(q, pos)
    k = rotary(k, pos) * (dk ** -0.5)
    log_gamma = jnp.log1p(-jnp.exp(jnp.linspace(math.log(1.0 / 32), math.log(1.0 / 512), h))).astype(jnp.float32)
    C = RET_CHUNK
    idx = jnp.arange(C, dtype=jnp.float32)
    diff = idx[:, None] - idx[None, :]
    intra = jnp.exp(jnp.where(diff[None] >= 0, diff[None] * log_gamma[:, None, None], -jnp.inf))
    cross_q = jnp.exp((idx[:, None] + 1.0) * log_gamma[None, :])[None, :, :, None]
    state_k = jnp.exp((C - 1.0 - idx[:, None]) * log_gamma[None, :])[None, :, :, None]
    chunk_decay = jnp.exp(C * log_gamma)[None, :, None, None]

    def step(R, inp):
        qc, kc, vc = inp
        scores = jnp.einsum('bihd,bjhd->bhij', qc, kc) * intra[None]
        inner = jnp.einsum('bhij,bjhe->bihe', scores, vc)
        cross = jnp.einsum('bihd,bhde->bihe', qc, R) * cross_q
        R = R * chunk_decay + jnp.einsum('bjhd,bjhe->bhde', kc * state_k, vc)
        return R, inner + cross

    R0 = jnp.zeros((b, h, dk, dv), jnp.float32)
    _, y = lax.scan(step, R0, (to_chunks(q, C), to_chunks(k, C), to_chunks(v, C)))
    y = from_chunks(y)
    mu = jnp.mean(y, axis=-1, keepdims=True)
    var = jnp.mean(jnp.square(y - mu), axis=-1, keepdims=True)
    y = (y - mu) * lax.rsqrt(var + EPS)
    out = jax.nn.silu(g) * y
    return out.reshape(b, s, h * dv).astype(dtype)


def causal_depthwise_conv(x, w, bias):
    ch = x.shape[-1]
    y = lax.conv_general_dilated(
        x, w[:, None, :], window_strides=(1,), padding=[(SSM_CONV - 1, 0)],
        dimension_numbers=('NWC', 'WIO', 'NWC'), feature_group_count=ch)
    return y + bias


def ssd_mixer(z, xbc, dt, conv_w, conv_b, dt_bias, a_log, d_skip, norm_w):
    dtype = z.dtype
    z, xbc, dt = (t.astype(jnp.float32) for t in (z, xbc, dt))
    conv_w, conv_b, dt_bias, a_log, d_skip, norm_w = (t.astype(jnp.float32) for t in (conv_w, conv_b, dt_bias, a_log, d_skip, norm_w))
    b, s, _ = xbc.shape
    G, R, P, N = SSM_GROUPS, SSM_HEADS // SSM_GROUPS, SSM_HEAD_DIM, SSM_STATE
    xbc = jax.nn.silu(causal_depthwise_conv(xbc, conv_w, conv_b))
    xs = xbc[..., :SSM_INNER].reshape(b, s, G, R, P)
    Bm = xbc[..., SSM_INNER:SSM_INNER + G * N].reshape(b, s, G, N)
    Cm = xbc[..., SSM_INNER + G * N:].reshape(b, s, G, N)
    dt = jax.nn.softplus(dt + dt_bias)
    A = -jnp.exp(a_log)
    dA = (dt * A).reshape(b, s, G, R)
    xdt = xs * dt.reshape(b, s, G, R, 1)
    L_ = SSM_CHUNK
    causal = jnp.tril(jnp.ones((L_, L_), dtype=bool))

    def step(state, inp):
        xc, ac, bc, cc = inp
        cs = jnp.cumsum(ac, axis=1)
        cst = jnp.moveaxis(cs, 1, -1)
        seg = cst[..., :, None] - cst[..., None, :]
        Lmat = jnp.exp(jnp.where(causal, seg, -jnp.inf))
        cb = jnp.einsum('blgn,bsgn->bgls', cc, bc)
        y_diag = jnp.einsum('bgls,bgrls,bsgrp->blgrp', cb, Lmat, xc)
        y_off = jnp.einsum('blgn,bgrpn->blgrp', cc, state) * jnp.exp(cs)[..., None]
        decay_to_end = jnp.exp(cs[:, -1:] - cs)
        state = state * jnp.exp(cs[:, -1])[..., None, None] + jnp.einsum('bsgn,bsgr,bsgrp->bgrpn', bc, decay_to_end, xc)
        return state, y_diag + y_off

    state0 = jnp.zeros((b, G, R, P, N), jnp.float32)
    _, y = lax.scan(step, state0, (to_chunks(xdt, L_), to_chunks(dA, L_), to_chunks(Bm, L_), to_chunks(Cm, L_)))
    y = from_chunks(y) + xs * d_skip.reshape(G, R, 1)
    y = y.reshape(b, s, SSM_INNER) * jax.nn.silu(z)
    yg = y.reshape(b, s, G, SSM_INNER // G)
    yg = yg * lax.rsqrt(jnp.mean(yg * yg, axis=-1, keepdims=True) + EPS)
    return (yg.reshape(b, s, SSM_INNER) * norm_w).astype(dtype)


def strided_band_attention(q, k, v, span, dil):
    b, s, h, e = q.shape
    L = s // dil
    nb = -(-L // span)
    pad = nb * span - L

    def strided_blocks(t):
        t = t.reshape(b, L, dil, h, e).transpose(0, 2, 1, 3, 4)
        t = jnp.pad(t, ((0, 0), (0, 0), (0, pad), (0, 0), (0, 0)))
        return t.reshape(b, dil, nb, span, h, e)

    qb, kb, vb = strided_blocks(q), strided_blocks(k), strided_blocks(v)

    def with_prev(t):
        prev = jnp.pad(t, ((0, 0), (0, 0), (1, 0), (0, 0), (0, 0), (0, 0)))[:, :, :-1]
        return jnp.concatenate([prev, t], axis=3)

    kk, vv = with_prev(kb), with_prev(vb)
    scores = jnp.einsum('bdnqhe,bdnkhe->bdnhqk', qb, kk) * (e ** -0.5)
    qi = jnp.arange(span)[:, None]
    kj = jnp.arange(2 * span)[None, :]
    dist = qi + span - kj
    band = (dist >= 0) & (dist <= span)
    key_pos = jnp.arange(nb)[:, None] * span - span + jnp.arange(2 * span)[None, :]
    valid = band[None] & (key_pos >= 0)[:, None, :]
    scores = jnp.where(valid[None, None, :, None], scores, -jnp.inf)
    m = jnp.max(scores, axis=-1, keepdims=True)
    lse = m + jnp.log(jnp.sum(jnp.exp(scores - m), axis=-1, keepdims=True))
    p = jnp.exp(scores - lse)
    o = jnp.einsum('bdnhqk,bdnkhe->bdnqhe', p, vv)
    o = o.reshape(b, dil, nb * span, h, e)[:, :, :L].transpose(0, 2, 1, 3, 4).reshape(b, s, h, e)
    lse = jnp.moveaxis(lse[..., 0], 3, 4)
    lse = lse.reshape(b, dil, nb * span, h)[:, :, :L].transpose(0, 2, 1, 3).reshape(b, s, h)
    return o, lse


def dilated_attention(q, k, v):
    dtype = v.dtype
    b, s, _ = q.shape
    n_g = len(ATT_PATTERNS)
    shp = (b, s, n_g, ATT_SLOTS, ATT_HEAD_DIM)
    q, k, v = (t.astype(jnp.float32).reshape(shp) for t in (q, k, v))
    outs, lses = [], []
    for gi, (window, dil) in enumerate(ATT_PATTERNS):
        o, lse = strided_band_attention(q[:, :, gi], k[:, :, gi], v[:, :, gi], window // dil, dil)
        outs.append(o)
        lses.append(lse)
    alpha = jax.nn.softmax(jnp.stack(lses, axis=0), axis=0)
    out = jnp.sum(alpha[..., None] * jnp.stack(outs, axis=0), axis=0)
    return out.reshape(b, s, ATT_SLOTS * ATT_HEAD_DIM).astype(dtype)


def setup_inputs(seed: int = 0) -> dict:
    key = jax.random.key(seed)
    ks = jax.random.split(key, 24)
    f32 = jnp.float32

    def nrm(k, shape, fan_in):
        return jax.random.normal(k, shape, f32) * (fan_in ** -0.5)

    def gain(k, shape):
        return 1.0 + 0.05 * jax.random.normal(k, shape, f32)

    dt0 = jnp.exp(jax.random.uniform(ks[8], (DEPTH, SSM_HEADS), f32, math.log(1e-3), math.log(1e-1)))
    return {
        "x": jax.random.normal(ks[0], (BATCH, SEQ, D_MODEL), f32),
        "c": jax.random.normal(ks[1], (BATCH, D_MODEL), f32),
        "ada_w": nrm(ks[2], (DEPTH, D_MODEL, 6 * D_MODEL), D_MODEL),
        "ada_b": 0.02 * jax.random.normal(ks[3], (DEPTH, 6 * D_MODEL), f32),
        "norm1_w": gain(ks[4], (DEPTH, D_MODEL)),
        "w_in": nrm(ks[5], (DEPTH, D_MODEL, IN_WIDTH), D_MODEL),
        "ssm_conv_w": nrm(ks[6], (DEPTH, SSM_CONV, SSM_XBC), SSM_CONV),
        "ssm_conv_b": 0.02 * jax.random.normal(ks[7], (DEPTH, SSM_XBC), f32),
        "ssm_dt_bias": dt0 + jnp.log(-jnp.expm1(-dt0)),
        "ssm_a_log": jnp.log(jax.random.uniform(ks[9], (DEPTH, SSM_HEADS), f32, 1.0, 16.0)),
        "ssm_d": 1.0 + 0.1 * jax.random.normal(ks[10], (DEPTH, SSM_HEADS), f32),
        "ssm_norm_w": gain(ks[11], (DEPTH, SSM_INNER)),
        "w_br_ret": nrm(ks[12], (DEPTH, RET_HEADS * RET_V_DIM, D_MODEL), RET_HEADS * RET_V_DIM),
        "w_br_ssm": nrm(ks[13], (DEPTH, SSM_INNER, D_MODEL), SSM_INNER),
        "w_br_att": nrm(ks[14], (DEPTH, ATT_SLOTS * ATT_HEAD_DIM, D_MODEL), ATT_SLOTS * ATT_HEAD_DIM),
        "w_out": nrm(ks[15], (DEPTH, D_MODEL, D_MODEL), D_MODEL),
        "norm2_w": gain(ks[16], (DEPTH, D_MODEL)),
        "w_ff1": nrm(ks[17], (DEPTH, D_MODEL, D_FF), D_MODEL),
        "w_ff3": nrm(ks[18], (DEPTH, D_MODEL, D_FF), D_MODEL),
        "w_ff2": nrm(ks[19], (DEPTH, D_FF, D_MODEL), D_FF),
        "final_norm_w": gain(ks[20], (D_MODEL,)),
    }


def reference(x, c, ada_w, ada_b, norm1_w, w_in, ssm_conv_w, ssm_conv_b, ssm_dt_bias, ssm_a_log,
              ssm_d, ssm_norm_w, w_br_ret, w_br_ssm, w_br_att, w_out, norm2_w, w_ff1, w_ff3, w_ff2,
              final_norm_w):
    b, s, _ = x.shape
    for i in range(DEPTH):
        ada = jax.nn.silu(c) @ ada_w[i] + ada_b[i]
        sh1, sc1, g1, sh2, sc2, g2 = jnp.split(ada[:, None, :], 6, axis=-1)

        h = rms_norm(x, norm1_w[i]) * (1.0 + sc1) + sh1
        proj = h @ w_in[i]
        (rq, rk, rv, rg, sz, sxbc, sdt, aq, ak, av, gate_logits) = jnp.split(proj, IN_SPLITS, axis=-1)
        y_ret = retention(rq.reshape(b, s, RET_HEADS, RET_QK_DIM), rk.reshape(b, s, RET_HEADS, RET_QK_DIM),
                          rv.reshape(b, s, RET_HEADS, RET_V_DIM), rg.reshape(b, s, RET_HEADS, RET_V_DIM))
        y_ssm = ssd_mixer(sz, sxbc, sdt, ssm_conv_w[i], ssm_conv_b[i], ssm_dt_bias[i], ssm_a_log[i],
                          ssm_d[i], ssm_norm_w[i])
        y_att = dilated_attention(aq, ak, av)
        gates = jax.nn.sigmoid(gate_logits).reshape(b, s, N_BRANCHES, D_MODEL)
        mixed = (gates[:, :, 0] * (y_ret @ w_br_ret[i])
                 + gates[:, :, 1] * (y_ssm @ w_br_ssm[i])
                 + gates[:, :, 2] * (y_att @ w_br_att[i]))
        x = x + g1 * (mixed @ w_out[i])

        h2 = rms_norm(x, norm2_w[i]) * (1.0 + sc2) + sh2
        ffn = (jax.nn.silu(h2 @ w_ff1[i]) * (h2 @ w_ff3[i])) @ w_ff2[i]
        x = x + g2 * ffn
    return rms_norm(x, final_norm_w)
```

```python
import functools
import math

import numpy as np
import jax
import jax.numpy as jnp
from jax import lax
from jax.experimental import pallas as pl
from jax.experimental.pallas import tpu as pltpu

F32 = jnp.float32
BF16 = jnp.bfloat16

D_MODEL = 1024
EPS = 1e-6
CHUNK = 128
RET_HEADS = 4
RET_QK = 128
RET_V = 256
ROPE_BASE = 10000.0
SSM_HEADS = 16
SSM_P = 64
SSM_G = 4
SSM_N = 128
SSM_CONV = 4
SSM_INNER = SSM_HEADS * SSM_P
ATT_PATTERNS = ((128, 1), (512, 4), (2048, 16))
ATT_SLOTS = 4
ATT_E = 64
ATT_W = len(ATT_PATTERNS) * ATT_SLOTS * ATT_E
N_BRANCHES = 3

_O_RQ = 0
_O_SXBC = 4096
_O_DT = 6144
_O_AQ = 6160
_O_GATE = 8464
_O_END = 11536
P_GATE = 0
P_RQ = 3072
P_RK = 3584
P_RV = 4096
P_RG = 5120
P_SZ = 6144
P_SX = 7168
P_SB = 8192
P_SC = 8704
P_AQ = 9216
P_AK = 9984
P_AV = 10752
P_WIDTH = 11520

LANES = 128
VMEM_LIMIT = 56 * 1024 * 1024


def _cparams(n_axes):
    return pltpu.CompilerParams(
        dimension_semantics=("arbitrary",) * n_axes, vmem_limit_bytes=VMEM_LIMIT)


def _silu(v):
    return v * jax.nn.sigmoid(v)


def _rms(v):
    return v * lax.rsqrt(jnp.mean(v * v, axis=-1, keepdims=True) + EPS)


def _ada_body(c_ref, w_ref, b_ref, o_ref):
    s = _silu(c_ref[...])
    o_ref[0] = jnp.dot(s.astype(BF16), w_ref[0], preferred_element_type=F32) + b_ref[0]


def _ada_call(c, ada_w, ada_b):
    depth, d, n = ada_w.shape
    b = c.shape[0]
    return pl.pallas_call(
        _ada_body,
        grid=(depth, n // d),
        in_specs=[pl.BlockSpec((b, d), lambda l, j: (0, 0)),
                  pl.BlockSpec((1, d, d), lambda l, j: (l, 0, j)),
                  pl.BlockSpec((1, 1, d), lambda l, j: (l, 0, j))],
        out_specs=pl.BlockSpec((1, b, d), lambda l, j: (l, 0, j)),
        out_shape=jax.ShapeDtypeStruct((depth, b, n), F32),
        compiler_params=_cparams(2),
        name="ada",
    )(c, ada_w, ada_b.reshape(depth, 1, n))


def _in_body(x_ref, nw_ref, sh_ref, sc_ref, w_ref, wdt_ref, o_ref, dt_ref, h_ref):
    @pl.when(pl.program_id(1) == 0)
    def _():
        h = _rms(x_ref[...]) * nw_ref[...]
        h = (h * (1.0 + sc_ref[0]) + sh_ref[0]).astype(BF16)
        h_ref[...] = h
        dt_ref[...] = jnp.dot(h, wdt_ref[...], preferred_element_type=F32)

    o_ref[...] = jnp.dot(h_ref[...], w_ref[...], preferred_element_type=F32).astype(BF16)


def _in_call(x2, nw, ada3, w_main, w_dt, *, seq, tm, tn):
    t, d = x2.shape
    n = w_main.shape[1]
    per_b = seq // tm
    return pl.pallas_call(
        _in_body,
        grid=(t // tm, n // tn),
        in_specs=[pl.BlockSpec((tm, d), lambda i, j: (i, 0)),
                  pl.BlockSpec((1, d), lambda i, j: (0, 0)),
                  pl.BlockSpec((1, 1, d), lambda i, j: (i // per_b, 0, 0)),
                  pl.BlockSpec((1, 1, d), lambda i, j: (i // per_b, 0, 1)),
                  pl.BlockSpec((d, tn), lambda i, j: (0, j)),
                  pl.BlockSpec((d, LANES), lambda i, j: (0, 0))],
        out_specs=[pl.BlockSpec((tm, tn), lambda i, j: (i, j)),
                   pl.BlockSpec((tm, LANES), lambda i, j: (i, 0))],
        out_shape=[jax.ShapeDtypeStruct((t, n), BF16),
                   jax.ShapeDtypeStruct((t, LANES), F32)],
        scratch_shapes=[pltpu.VMEM((tm, d), BF16)],
        compiler_params=_cparams(2),
        name="in_proj",
    )(x2, nw, ada3, ada3, w_main, w_dt)


def _ret_tables(seq):
    half = RET_QK // 2
    inv_freq = ROPE_BASE ** (-np.arange(half, dtype=np.float64) / half)
    ang = np.arange(seq, dtype=np.float64)[:, None] * inv_freq[None, :]
    cos, sin = np.cos(ang), np.sin(ang)
    cos2 = np.concatenate([cos, cos], axis=1)
    sin2 = np.concatenate([-sin, sin], axis=1)
    lg = np.log1p(-np.exp(np.linspace(math.log(1.0 / 32), math.log(1.0 / 512), RET_HEADS)))
    idx = np.arange(CHUNK, dtype=np.float64)
    diff = idx[:, None] - idx[None, :]
    intra = np.where(diff[None] >= 0, np.exp(diff[None] * lg[:, None, None]), 0.0)
    cross_q = np.exp((idx[None, :, None] + 1.0) * lg[:, None, None])
    state_k = np.exp((CHUNK - 1.0 - idx[None, :, None]) * lg[:, None, None])
    chunk_decay = np.exp(CHUNK * lg)[:, None, None]
    as32 = lambda a: jnp.asarray(a, F32)
    return (as32(cos2), as32(sin2), as32(intra),
            as32(np.broadcast_to(cross_q, (RET_HEADS, CHUNK, RET_V))),
            as32(np.broadcast_to(state_k, (RET_HEADS, CHUNK, RET_QK))),
            as32(np.broadcast_to(chunk_decay, (RET_HEADS, 1, RET_V))))


def _ret_body(q_ref, k_ref, v_ref, g_ref, cos_ref, sin_ref, intra_ref, cq_ref, sk_ref, cd_ref,
              o_ref, r_ref):
    n_chunks = q_ref.shape[0] // CHUNK
    r_ref[...] = jnp.zeros_like(r_ref)
    intra = intra_ref[0]
    cross_q = cq_ref[0]
    state_k = sk_ref[0]
    chunk_decay = cd_ref[0]

    def step(c, carry):
        rows = pl.ds(pl.multiple_of(c * CHUNK, CHUNK), CHUNK)
        cos = cos_ref[rows, :]
        sin = sin_ref[rows, :]
        q = q_ref[rows, :].astype(F32)
        k = k_ref[rows, :].astype(F32)
        qr = q * cos + pltpu.roll(q, RET_QK // 2, 1) * sin
        kr = (k * cos + pltpu.roll(k, RET_QK // 2, 1) * sin) * (RET_QK ** -0.5)
        qb = qr.astype(BF16)
        v = v_ref[rows, :]
        scores = lax.dot_general(qb, kr.astype(BF16), (((1,), (1,)), ((), ())),
                                 preferred_element_type=F32) * intra
        inner = jnp.dot(scores.astype(BF16), v, preferred_element_type=F32)
        state = r_ref[...]
        cross = jnp.dot(qb, state.astype(BF16), preferred_element_type=F32) * cross_q
        ks_t = (kr * state_k).T.astype(BF16)
        r_ref[...] = state * chunk_decay + jnp.dot(ks_t, v, preferred_element_type=F32)
        y = inner + cross
        dlt = y - jnp.mean(y, axis=-1, keepdims=True)
        yn = dlt * lax.rsqrt(jnp.mean(dlt * dlt, axis=-1, keepdims=True) + EPS)
        o_ref[rows, :] = (_silu(g_ref[rows, :].astype(F32)) * yn).astype(BF16)
        return carry

    lax.fori_loop(0, n_chunks, step, 0)


def _ret_call(proj, tables, *, batch, seq):
    cos2, sin2, intra, cq, sk, cd = tables
    t = proj.shape[0]
    qk0, k0, v0, g0 = P_RQ // RET_QK, P_RK // RET_QK, P_RV // RET_V, P_RG // RET_V
    return pl.pallas_call(
        _ret_body,
        grid=(batch, RET_HEADS),
        in_specs=[pl.BlockSpec((seq, RET_QK), lambda b, h: (b, qk0 + h)),
                  pl.BlockSpec((seq, RET_QK), lambda b, h: (b, k0 + h)),
                  pl.BlockSpec((seq, RET_V), lambda b, h: (b, v0 + h)),
                  pl.BlockSpec((seq, RET_V), lambda b, h: (b, g0 + h)),
                  pl.BlockSpec((seq, RET_QK), lambda b, h: (0, 0)),
                  pl.BlockSpec((seq, RET_QK), lambda b, h: (0, 0)),
                  pl.BlockSpec((1, CHUNK, CHUNK), lambda b, h: (h, 0, 0)),
                  pl.BlockSpec((1, CHUNK, RET_V), lambda b, h: (h, 0, 0)),
                  pl.BlockSpec((1, CHUNK, RET_QK), lambda b, h: (h, 0, 0)),
                  pl.BlockSpec((1, 1, RET_V), lambda b, h: (h, 0, 0))],
        out_specs=pl.BlockSpec((seq, RET_V), lambda b, h: (b, h)),
        out_shape=jax.ShapeDtypeStruct((t, RET_HEADS * RET_V), BF16),
        scratch_shapes=[pltpu.VMEM((RET_QK, RET_V), F32)],
        compiler_params=_cparams(2),
        name="retention",
    )(proj, proj, proj, proj, cos2, sin2, intra, cq, sk, cd)


def _pair(col_src, h0, shape):
    lane = lax.broadcasted_iota(jnp.int32, shape, 1)
    a = jnp.broadcast_to(col_src[:, h0:h0 + 1], shape)
    b = jnp.broadcast_to(col_src[:, h0 + 1:h0 + 2], shape)
    return jnp.where(lane < SSM_P, a, b)


def _ssd_body(z_ref, xs_ref, bm_ref, cm_ref, dt_ref, cwx_ref, cwb_ref, cwc_ref, cbx_ref, cbb_ref,
              cbc_ref, dtb_ref, alog_ref, dsk_ref, nw_ref, o_ref, st_ref, tx_ref, tb_ref, tc_ref):
    n_chunks = z_ref.shape[0] // CHUNK
    st_ref[...] = jnp.zeros_like(st_ref)
    tx_ref[...] = jnp.zeros_like(tx_ref)
    tb_ref[...] = jnp.zeros_like(tb_ref)
    tc_ref[...] = jnp.zeros_like(tc_ref)
    row = lax.broadcasted_iota(jnp.int32, (CHUNK, CHUNK), 0)
    col = lax.broadcasted_iota(jnp.int32, (CHUNK, CHUNK), 1)
    causal = row >= col
    lower = causal.astype(F32)
    upper = (row <= col).astype(F32)
    lane = lax.broadcasted_iota(jnp.int32, (CHUNK, LANES), 1)
    first = lane < SSM_P
    hi = lax.Precision.HIGHEST

    def conv(x_ref, tail_ref, w_ref, b_ref, rows):
        cur = x_ref[rows, :].astype(F32)
        cat = jnp.concatenate([tail_ref[...], cur], axis=0)
        acc = b_ref[...]
        for k in range(SSM_CONV):
            lo = 8 - (SSM_CONV - 1) + k
            acc = acc + cat[lo:lo + CHUNK, :] * w_ref[k:k + 1, :]
        tail_ref[...] = cur[CHUNK - 8:, :]
        return _silu(acc)

    def step(c, carry):
        rows = pl.ds(pl.multiple_of(c * CHUNK, CHUNK), CHUNK)
        xs = conv(xs_ref, tx_ref, cwx_ref, cbx_ref, rows)
        bm = conv(bm_ref, tb_ref, cwb_ref, cbb_ref, rows)
        cm = conv(cm_ref, tc_ref, cwc_ref, cbc_ref, rows)
        dt_raw = dt_ref[rows, :] + dtb_ref[...]
        dt = jnp.maximum(dt_raw, 0.0) + jnp.log(1.0 + jnp.exp(-jnp.abs(dt_raw)))
        da = dt * (-jnp.exp(alog_ref[...]))
        cs_col = jnp.dot(lower, da, precision=hi, preferred_element_type=F32)
        cs_row = jnp.dot(da.T, upper, precision=hi, preferred_element_type=F32)
        cs_last = cs_col[CHUNK - 1:CHUNK, :]
        to_end = jnp.exp(cs_last - cs_col)
        from_start = jnp.exp(cs_col)
        end_decay = jnp.exp(cs_last)
        dsk = dsk_ref[...]
        for g in range(SSM_G):
            bg = bm[:, g * SSM_N:(g + 1) * SSM_N]
            cg = cm[:, g * SSM_N:(g + 1) * SSM_N].astype(BF16)
            cb = lax.dot_general(cg, bg.astype(BF16), (((1,), (1,)), ((), ())),
                                 preferred_element_type=F32)
            bg_t = bg.T.astype(BF16)
            ssq = jnp.zeros((CHUNK, 1), F32)
            ys = []
            for pr in range(2):
                h0 = g * 4 + pr * 2
                blk = slice(h0 * SSM_P, (h0 + 2) * SSM_P)
                xh = xs[:, blk]
                xdt = xh * _pair(dt, h0, (CHUNK, LANES))
                xdt_b = xdt.astype(BF16)
                yd = []
                for hh in (h0, h0 + 1):
                    seg = cs_col[:, hh:hh + 1] - cs_row[hh:hh + 1, :]
                    lmat = jnp.exp(jnp.where(causal, seg, -jnp.inf))
                    yd.append(jnp.dot((cb * lmat).astype(BF16), xdt_b, preferred_element_type=F32))
                y_diag = jnp.where(first, yd[0], yd[1])
                state = st_ref[g, pr]
                y_off = jnp.dot(cg, state.astype(BF16), preferred_element_type=F32)
                y_off = y_off * _pair(from_start, h0, (CHUNK, LANES))
                upd = (xdt * _pair(to_end, h0, (CHUNK, LANES))).astype(BF16)
                st_ref[g, pr] = (state * _pair(end_decay, h0, (1, LANES))
                                 + jnp.dot(bg_t, upd, preferred_element_type=F32))
                y = y_diag + y_off + xh * _pair(dsk, h0, (1, LANES))
                y = y * _silu(z_ref[rows, blk].astype(F32))
                ssq = ssq + jnp.sum(y * y, axis=-1, keepdims=True)
                ys.append(y)
            inv = lax.rsqrt(ssq * (1.0 / (4 * SSM_P)) + EPS)
            for pr in range(2):
                h0 = g * 4 + pr * 2
                blk = slice(h0 * SSM_P, (h0 + 2) * SSM_P)
                o_ref[rows, blk] = (ys[pr] * inv * nw_ref[:, blk]).astype(BF16)
        return carry

    lax.fori_loop(0, n_chunks, step, 0)


def _ssd_call(proj, dt, conv_w, conv_b, dt_bias, a_log, d_skip, norm_w, *, batch, seq):
    t = proj.shape[0]
    gn = SSM_G * SSM_N
    pad = lambda v: jnp.pad(v, (0, LANES - SSM_HEADS)).reshape(1, LANES)
    conv_b2 = conv_b.reshape(1, -1)
    return pl.pallas_call(
        _ssd_body,
        grid=(batch,),
        in_specs=[pl.BlockSpec((seq, SSM_INNER), lambda b: (b, P_SZ // SSM_INNER)),
                  pl.BlockSpec((seq, SSM_INNER), lambda b: (b, P_SX // SSM_INNER)),
                  pl.BlockSpec((seq, gn), lambda b: (b, P_SB // gn)),
                  pl.BlockSpec((seq, gn), lambda b: (b, P_SC // gn)),
                  pl.BlockSpec((seq, LANES), lambda b: (b, 0)),
                  pl.BlockSpec((SSM_CONV, SSM_INNER), lambda b: (0, 0)),
                  pl.BlockSpec((SSM_CONV, gn), lambda b: (0, SSM_INNER // gn)),
                  pl.BlockSpec((SSM_CONV, gn), lambda b: (0, SSM_INNER // gn + 1)),
                  pl.BlockSpec((1, SSM_INNER), lambda b: (0, 0)),
                  pl.BlockSpec((1, gn), lambda b: (0, SSM_INNER // gn)),
                  pl.BlockSpec((1, gn), lambda b: (0, SSM_INNER // gn + 1)),
                  pl.BlockSpec((1, LANES), lambda b: (0, 0)),
                  pl.BlockSpec((1, LANES), lambda b: (0, 0)),
                  pl.BlockSpec((1, LANES), lambda b: (0, 0)),
                  pl.BlockSpec((1, SSM_INNER), lambda b: (0, 0))],
        out_specs=pl.BlockSpec((seq, SSM_INNER), lambda b: (b, 0)),
        out_shape=jax.ShapeDtypeStruct((t, SSM_INNER), BF16),
        scratch_shapes=[pltpu.VMEM((SSM_G, 2, SSM_N, LANES), F32),
                        pltpu.VMEM((8, SSM_INNER), F32),
                        pltpu.VMEM((8, gn), F32),
                        pltpu.VMEM((8, gn), F32)],
        compiler_params=_cparams(1),
        name="ssd",
    )(proj, proj, proj, proj, dt, conv_w, conv_w, conv_w, conv_b2, conv_b2, conv_b2,
      pad(dt_bias), pad(a_log), pad(d_skip), norm_w.reshape(1, -1))


def _att_body(q_ref, k_ref, v_ref, o_ref, m_ref, l_ref, acc_ref):
    qi = pl.program_id(1)
    m_ref[...] = jnp.full_like(m_ref, -jnp.inf)
    l_ref[...] = jnp.zeros_like(l_ref)
    acc_ref[...] = jnp.zeros_like(acc_ref)
    base = (lax.broadcasted_iota(jnp.int32, (CHUNK, CHUNK), 0)
            - lax.broadcasted_iota(jnp.int32, (CHUNK, CHUNK), 1))
    for gi, (window, dil) in enumerate(ATT_PATTERNS):
        n_blocks = jnp.minimum(qi + 1, window // CHUNK + 1)

        def block(j, carry, gi=gi, window=window, dil=dil):
            rows = pl.ds(pl.multiple_of((qi - j) * CHUNK, CHUNK), CHUNK)
            dist = base + j * CHUNK
            valid = (dist >= 0) & (dist <= window) & ((dist & (dil - 1)) == 0)
            for s in range(ATT_SLOTS):
                c0 = (gi * ATT_SLOTS + s) * ATT_E
                cols = slice(c0, c0 + ATT_E)
                sc = lax.dot_general(q_ref[:, cols], k_ref[rows, cols], (((1,), (1,)), ((), ())),
                                     preferred_element_type=F32) * (ATT_E ** -0.5)
                sc = jnp.where(valid, sc, -jnp.inf)
                m_old = m_ref[s]
                m_new = jnp.maximum(m_old, jnp.max(sc, axis=-1, keepdims=True))
                alpha = jnp.exp(m_old - m_new)
                p = jnp.exp(sc - m_new)
                l_ref[s] = alpha * l_ref[s] + jnp.sum(p, axis=-1, keepdims=True)
                acc_ref[s] = alpha * acc_ref[s] + jnp.dot(p.astype(BF16), v_ref[rows, cols],
                                                          preferred_element_type=F32)
                m_ref[s] = m_new
            return carry

        lax.fori_loop(0, n_blocks, block, 0)
    for s in range(ATT_SLOTS):
        o_ref[:, s * ATT_E:(s + 1) * ATT_E] = (acc_ref[s] / l_ref[s]).astype(BF16)


def _att_call(proj, *, batch, seq):
    t = proj.shape[0]
    nq = seq // CHUNK
    return pl.pallas_call(
        _att_body,
        grid=(batch, nq),
        in_specs=[pl.BlockSpec((CHUNK, ATT_W), lambda b, i: (b * nq + i, P_AQ // ATT_W)),
                  pl.BlockSpec((seq, ATT_W), lambda b, i: (b, P_AK // ATT_W)),
                  pl.BlockSpec((seq, ATT_W), lambda b, i: (b, P_AV // ATT_W))],
        out_specs=pl.BlockSpec((CHUNK, ATT_SLOTS * ATT_E), lambda b, i: (b * nq + i, 0)),
        out_shape=jax.ShapeDtypeStruct((t, ATT_SLOTS * ATT_E), BF16),
        scratch_shapes=[pltpu.VMEM((ATT_SLOTS, CHUNK, 1), F32),
                        pltpu.VMEM((ATT_SLOTS, CHUNK, 1), F32),
                        pltpu.VMEM((ATT_SLOTS, CHUNK, ATT_E), F32)],
        compiler_params=_cparams(2),
        name="dilated_attention",
    )(proj, proj, proj)


def _merge_body(x_ref, g0_ref, g1_ref, g2_ref, yr_ref, ys_ref, ya_ref, gate_ref,
                wr_ref, ws_ref, wa_ref, wo_ref, o_ref):
    def branch(g_ref, y_ref, w_ref):
        return jax.nn.sigmoid(g_ref[...].astype(F32)) * jnp.dot(
            y_ref[...], w_ref[...], preferred_element_type=F32)

    mixed = (branch(g0_ref, yr_ref, wr_ref) + branch(g1_ref, ys_ref, ws_ref)
             + branch(g2_ref, ya_ref, wa_ref))
    upd = jnp.dot(mixed.astype(BF16), wo_ref[...], preferred_element_type=F32)
    o_ref[...] = x_ref[...] + gate_ref[0] * upd


def _merge_call(x2, proj, y_ret, y_ssm, y_att, ada3, wr, ws, wa, wo, *, seq, tm):
    t, d = x2.shape
    per_b = seq // tm
    const = lambda i: (0, 0)
    return pl.pallas_call(
        _merge_body,
        grid=(t // tm,),
        in_specs=[pl.BlockSpec((tm, d), lambda i: (i, 0)),
                  pl.BlockSpec((tm, d), lambda i: (i, P_GATE // d)),
                  pl.BlockSpec((tm, d), lambda i: (i, P_GATE // d + 1)),
                  pl.BlockSpec((tm, d), lambda i: (i, P_GATE // d + 2)),
                  pl.BlockSpec((tm, y_ret.shape[1]), lambda i: (i, 0)),
                  pl.BlockSpec((tm, y_ssm.shape[1]), lambda i: (i, 0)),
                  pl.BlockSpec((tm, y_att.shape[1]), lambda i: (i, 0)),
                  pl.BlockSpec((1, 1, d), lambda i: (i // per_b, 0, 2)),
                  pl.BlockSpec(wr.shape, const),
                  pl.BlockSpec(ws.shape, const),
                  pl.BlockSpec(wa.shape, const),
                  pl.BlockSpec(wo.shape, const)],
        out_specs=pl.BlockSpec((tm, d), lambda i: (i, 0)),
        out_shape=jax.ShapeDtypeStruct((t, d), F32),
        compiler_params=_cparams(1),
        name="merge_out",
    )(x2, proj, proj, proj, y_ret, y_ssm, y_att, ada3, wr, ws, wa, wo)


def _ffn_body(x_ref, nw_ref, sh_ref, sc_ref, gate_ref, w1_ref, w3_ref, w2_ref, fw_ref, o_ref,
              *, final_norm):
    x = x_ref[...]
    h = _rms(x) * nw_ref[...]
    h = (h * (1.0 + sc_ref[0]) + sh_ref[0]).astype(BF16)
    a = jnp.dot(h, w1_ref[...], preferred_element_type=F32)
    b = jnp.dot(h, w3_ref[...], preferred_element_type=F32)
    act = (_silu(a) * b).astype(BF16)
    y = x + gate_ref[0] * jnp.dot(act, w2_ref[...], preferred_element_type=F32)
    if final_norm:
        y = _rms(y) * fw_ref[...]
    o_ref[...] = y


def _ffn_call(x2, nw, ada3, w1, w3, w2, final_w, *, seq, tm, final_norm):
    t, d = x2.shape
    per_b = seq // tm
    const = lambda i: (0, 0)
    resident = dict(pipeline_mode=pl.Buffered(1))
    return pl.pallas_call(
        functools.partial(_ffn_body, final_norm=final_norm),
        grid=(t // tm,),
        in_specs=[pl.BlockSpec((tm, d), lambda i: (i, 0)),
                  pl.BlockSpec((1, d), const),
                  pl.BlockSpec((1, 1, d), lambda i: (i // per_b, 0, 3)),
                  pl.BlockSpec((1, 1, d), lambda i: (i // per_b, 0, 4)),
                  pl.BlockSpec((1, 1, d), lambda i: (i // per_b, 0, 5)),
                  pl.BlockSpec(w1.shape, const, **resident),
                  pl.BlockSpec(w3.shape, const, **resident),
                  pl.BlockSpec(w2.shape, const, **resident),
                  pl.BlockSpec((1, d), const)],
        out_specs=pl.BlockSpec((tm, d), lambda i: (i, 0)),
        out_shape=jax.ShapeDtypeStruct((t, d), F32),
        compiler_params=_cparams(1),
        name="ffn",
    )(x2, nw, ada3, ada3, ada3, w1, w3, w2, final_w)


def _tile(seq, want):
    return min(seq, want)


def kernel(x, c, ada_w, ada_b, norm1_w, w_in, ssm_conv_w, ssm_conv_b, ssm_dt_bias, ssm_a_log,
           ssm_d, ssm_norm_w, w_br_ret, w_br_ssm, w_br_att, w_out, norm2_w, w_ff1, w_ff3, w_ff2,
           final_norm_w):
    batch, seq, d = x.shape
    depth = ada_w.shape[0]
    assert d == D_MODEL and seq % CHUNK == 0 and w_in.shape[2] == _O_END
    t = batch * seq
    ada = _ada_call(c, ada_w.astype(BF16), ada_b)
    tables = _ret_tables(seq)
    x2 = x.reshape(t, d)
    final_w = final_norm_w.reshape(1, d)
    for i in range(depth):
        ada3 = ada[i].reshape(batch, 1, 6 * d)
        w = w_in[i]
        w_main = jnp.concatenate([w[:, _O_GATE:_O_END], w[:, :_O_DT], w[:, _O_AQ:_O_GATE]],
                                 axis=1).astype(BF16)
        w_dt = jnp.pad(w[:, _O_DT:_O_AQ], ((0, 0), (0, LANES - SSM_HEADS))).astype(BF16)
        proj, dt = _in_call(x2, norm1_w[i].reshape(1, d), ada3, w_main, w_dt,
                            seq=seq, tm=_tile(seq, 1024), tn=1280)
        y_ret = _ret_call(proj, tables, batch=batch, seq=seq)
        y_ssm = _ssd_call(proj, dt, ssm_conv_w[i], ssm_conv_b[i], ssm_dt_bias[i], ssm_a_log[i],
                          ssm_d[i], ssm_norm_w[i], batch=batch, seq=seq)
        y_att = _att_call(proj, batch=batch, seq=seq)
        x2 = _merge_call(x2, proj, y_ret, y_ssm, y_att, ada3, w_br_ret[i].astype(BF16),
                         w_br_ssm[i].astype(BF16), w_br_att[i].astype(BF16),
                         w_out[i].astype(BF16), seq=seq, tm=_tile(seq, 512))
        x2 = _ffn_call(x2, norm2_w[i].reshape(1, d), ada3, w_ff1[i].astype(BF16),
                       w_ff3[i].astype(BF16), w_ff2[i].astype(BF16), final_w,
                       seq=seq, tm=_tile(seq, 512), final_norm=(i == depth - 1))
    return x2.reshape(batch, seq, d)
```

```python
import functools
import math

import numpy as np
import jax
import jax.numpy as jnp
from jax import lax
from jax.experimental import pallas as pl
from jax.experimental.pallas import tpu as pltpu

F32 = jnp.float32
BF16 = jnp.bfloat16

D_MODEL = 1024
EPS = 1e-6
CHUNK = 128
RET_HEADS = 4
RET_QK = 128
RET_V = 256
ROPE_BASE = 10000.0
SSM_HEADS = 16
SSM_P = 64
SSM_G = 4
SSM_N = 128
SSM_CONV = 4
SSM_INNER = SSM_HEADS * SSM_P
ATT_PATTERNS = ((128, 1), (512, 4), (2048, 16))
ATT_SLOTS = 4
ATT_E = 64
ATT_W = len(ATT_PATTERNS) * ATT_SLOTS * ATT_E
N_BRANCHES = 3

_O_RQ = 0
_O_SXBC = 4096
_O_DT = 6144
_O_AQ = 6160
_O_GATE = 8464
_O_END = 11536
P_GATE = 0
P_RQ = 3072
P_RK = 3584
P_RV = 4096
P_RG = 5120
P_SZ = 6144
P_SX = 7168
P_SB = 8192
P_SC = 8704
P_AQ = 9216
P_AK = 9984
P_AV = 10752
P_WIDTH = 11520

LANES = 128
VMEM_LIMIT = 56 * 1024 * 1024


def _cparams(n_axes):
    return pltpu.CompilerParams(
        dimension_semantics=("arbitrary",) * n_axes, vmem_limit_bytes=VMEM_LIMIT)


def _silu(v):
    return v * jax.nn.sigmoid(v)


def _rms(v):
    return v * lax.rsqrt(jnp.mean(v * v, axis=-1, keepdims=True) + EPS)


def _ada_body(c_ref, w_ref, b_ref, o_ref):
    s = _silu(c_ref[...])
    o_ref[0] = jnp.dot(s.astype(BF16), w_ref[0], preferred_element_type=F32) + b_ref[0]


def _ada_call(c, ada_w, ada_b):
    depth, d, n = ada_w.shape
    b = c.shape[0]
    return pl.pallas_call(
        _ada_body,
        grid=(depth, n // d),
        in_specs=[pl.BlockSpec((b, d), lambda l, j: (0, 0)),
                  pl.BlockSpec((1, d, d), lambda l, j: (l, 0, j)),
                  pl.BlockSpec((1, 1, d), lambda l, j: (l, 0, j))],
        out_specs=pl.BlockSpec((1, b, d), lambda l, j: (l, 0, j)),
        out_shape=jax.ShapeDtypeStruct((depth, b, n), F32),
        compiler_params=_cparams(2),
        name="ada",
    )(c, ada_w, ada_b.reshape(depth, 1, n))


def _in_body(x_ref, nw_ref, sh_ref, sc_ref, w_ref, wdt_ref, o_ref, dt_ref, h_ref):
    @pl.when(pl.program_id(1) == 0)
    def _():
        h = _rms(x_ref[...]) * nw_ref[...]
        h = (h * (1.0 + sc_ref[0]) + sh_ref[0]).astype(BF16)
        h_ref[...] = h
        dt_ref[...] = jnp.dot(h, wdt_ref[...], preferred_element_type=F32)

    o_ref[...] = jnp.dot(h_ref[...], w_ref[...], preferred_element_type=F32).astype(BF16)


def _in_call(x2, nw, ada3, w_main, w_dt, *, seq, tm, tn):
    t, d = x2.shape
    n = w_main.shape[1]
    per_b = seq // tm
    return pl.pallas_call(
        _in_body,
        grid=(t // tm, n // tn),
        in_specs=[pl.BlockSpec((tm, d), lambda i, j: (i, 0)),
                  pl.BlockSpec((1, d), lambda i, j: (0, 0)),
                  pl.BlockSpec((1, 1, d), lambda i, j: (i // per_b, 0, 0)),
                  pl.BlockSpec((1, 1, d), lambda i, j: (i // per_b, 0, 1)),
                  pl.BlockSpec((d, tn), lambda i, j: (0, j)),
                  pl.BlockSpec((d, LANES), lambda i, j: (0, 0))],
        out_specs=[pl.BlockSpec((tm, tn), lambda i, j: (i, j)),
                   pl.BlockSpec((tm, LANES), lambda i, j: (i, 0))],
        out_shape=[jax.ShapeDtypeStruct((t, n), BF16),
                   jax.ShapeDtypeStruct((t, LANES), F32)],
        scratch_shapes=[pltpu.VMEM((tm, d), BF16)],
        compiler_params=_cparams(2),
        name="in_proj",
    )(x2, nw, ada3, ada3, w_main, w_dt)


def _ret_tables(seq):
    half = RET_QK // 2
    inv_freq = ROPE_BASE ** (-np.arange(half, dtype=np.float64) / half)
    ang = np.arange(seq, dtype=np.float64)[:, None] * inv_freq[None, :]
    cos, sin = np.cos(ang), np.sin(ang)
    cos2 = np.concatenate([cos, cos], axis=1)
    sin2 = np.concatenate([-sin, sin], axis=1)
    lg = np.log1p(-np.exp(np.linspace(math.log(1.0 / 32), math.log(1.0 / 512), RET_HEADS)))
    idx = np.arange(CHUNK, dtype=np.float64)
    diff = idx[:, None] - idx[None, :]
    intra = np.where(diff[None] >= 0, np.exp(diff[None] * lg[:, None, None]), 0.0)
    cross_q = np.exp((idx[None, :, None] + 1.0) * lg[:, None, None])
    state_k = np.exp((CHUNK - 1.0 - idx[None, :, None]) * lg[:, None, None])
    chunk_decay = np.exp(CHUNK * lg)[:, None, None]
    as32 = lambda a: jnp.asarray(a, F32)
    return (as32(cos2), as32(sin2), as32(intra),
            as32(np.broadcast_to(cross_q, (RET_HEADS, CHUNK, RET_V))),
            as32(np.broadcast_to(state_k, (RET_HEADS, CHUNK, RET_QK))),
            as32(np.broadcast_to(chunk_decay, (RET_HEADS, 1, RET_V))))


def _ret_body(q_ref, k_ref, v_ref, g_ref, cos_ref, sin_ref, intra_ref, cq_ref, sk_ref, cd_ref,
              o_ref, r_ref):
    n_chunks = q_ref.shape[0] // CHUNK
    r_ref[...] = jnp.zeros_like(r_ref)

    def step(c, carry):
        rows = pl.ds(pl.multiple_of(c * CHUNK, CHUNK), CHUNK)
        cos = cos_ref[rows, :]
        sin = sin_ref[rows, :]
        for h in range(RET_HEADS):
            qk_cols = slice(h * RET_QK, (h + 1) * RET_QK)
            v_cols = slice(h * RET_V, (h + 1) * RET_V)
            q = q_ref[rows, qk_cols].astype(F32)
            k = k_ref[rows, qk_cols].astype(F32)
            qr = q * cos + pltpu.roll(q, RET_QK // 2, 1) * sin
            kr = (k * cos + pltpu.roll(k, RET_QK // 2, 1) * sin) * (RET_QK ** -0.5)
            qb = qr.astype(BF16)
            v = v_ref[rows, v_cols]
            scores = lax.dot_general(qb, kr.astype(BF16), (((1,), (1,)), ((), ())),
                                     preferred_element_type=F32) * intra_ref[h]
            inner = jnp.dot(scores.astype(BF16), v, preferred_element_type=F32)
            state = r_ref[h]
            cross = jnp.dot(qb, state.astype(BF16), preferred_element_type=F32) * cq_ref[h]
            ks_t = (kr * sk_ref[h]).T.astype(BF16)
            r_ref[h] = state * cd_ref[h] + jnp.dot(ks_t, v, preferred_element_type=F32)
            y = inner + cross
            dlt = y - jnp.mean(y, axis=-1, keepdims=True)
            yn = dlt * lax.rsqrt(jnp.mean(dlt * dlt, axis=-1, keepdims=True) + EPS)
            o_ref[rows, v_cols] = (_silu(g_ref[rows, v_cols].astype(F32)) * yn).astype(BF16)
        return carry

    lax.fori_loop(0, n_chunks, step, 0)


def _ret_call(proj, tables, *, batch, seq):
    cos2, sin2, intra, cq, sk, cd = tables
    t = proj.shape[0]
    wqk, wv = RET_HEADS * RET_QK, RET_HEADS * RET_V
    whole = lambda a: pl.BlockSpec(a.shape, lambda b: (0,) * a.ndim)
    return pl.pallas_call(
        _ret_body,
        grid=(batch,),
        in_specs=[pl.BlockSpec((seq, wqk), lambda b: (b, P_RQ // wqk)),
                  pl.BlockSpec((seq, wqk), lambda b: (b, P_RK // wqk)),
                  pl.BlockSpec((seq, wv), lambda b: (b, P_RV // wv)),
                  pl.BlockSpec((seq, wv), lambda b: (b, P_RG // wv)),
                  whole(cos2), whole(sin2), whole(intra), whole(cq), whole(sk), whole(cd)],
        out_specs=pl.BlockSpec((seq, wv), lambda b: (b, 0)),
        out_shape=jax.ShapeDtypeStruct((t, wv), BF16),
        scratch_shapes=[pltpu.VMEM((RET_HEADS, RET_QK, RET_V), F32)],
        compiler_params=_cparams(1),
        name="retention",
    )(proj, proj, proj, proj, cos2, sin2, intra, cq, sk, cd)


def _pair(col_src, h0, shape):
    lane = lax.broadcasted_iota(jnp.int32, shape, 1)
    a = jnp.broadcast_to(col_src[:, h0:h0 + 1], shape)
    b = jnp.broadcast_to(col_src[:, h0 + 1:h0 + 2], shape)
    return jnp.where(lane < SSM_P, a, b)


def _ssd_body(z_ref, xs_ref, bm_ref, cm_ref, dt_ref, cwx_ref, cwb_ref, cwc_ref, cbx_ref, cbb_ref,
              cbc_ref, dtb_ref, alog_ref, dsk_ref, nw_ref, o_ref, st_ref):
    n_chunks = z_ref.shape[0] // CHUNK
    st_ref[...] = jnp.zeros_like(st_ref)
    row = lax.broadcasted_iota(jnp.int32, (CHUNK, CHUNK), 0)
    col = lax.broadcasted_iota(jnp.int32, (CHUNK, CHUNK), 1)
    causal = row >= col
    lower = causal.astype(F32)
    upper = (row <= col).astype(F32)
    lane = lax.broadcasted_iota(jnp.int32, (CHUNK, LANES), 1)
    first = lane < SSM_P
    hi = lax.Precision.HIGHEST

    wrow = lax.broadcasted_iota(jnp.int32, (CHUNK, 2 * CHUNK), 0)
    wcol = lax.broadcasted_iota(jnp.int32, (CHUNK, 2 * CHUNK), 1)

    def conv(x_ref, w_ref, b_ref, rows, win, shifts):
        window = x_ref[win, :]
        acc = b_ref[...] + x_ref[rows, :].astype(F32) * w_ref[SSM_CONV - 1:SSM_CONV, :]
        for k in range(SSM_CONV - 1):
            acc = acc + jnp.dot(shifts[k], window, preferred_element_type=F32) * w_ref[k:k + 1, :]
        return _silu(acc)

    def step(c, carry):
        rows = pl.ds(pl.multiple_of(c * CHUNK, CHUNK), CHUNK)
        first_chunk = c == 0
        win = pl.ds(pl.multiple_of(jnp.maximum(c - 1, 0) * CHUNK, CHUNK), 2 * CHUNK)
        cur_off = jnp.where(first_chunk, 0, CHUNK)
        shifts = [jnp.where(wcol == wrow + (cur_off - (SSM_CONV - 1 - k)), 1.0, 0.0).astype(BF16)
                  for k in range(SSM_CONV - 1)]
        xs = conv(xs_ref, cwx_ref, cbx_ref, rows, win, shifts)
        bm = conv(bm_ref, cwb_ref, cbb_ref, rows, win, shifts)
        cm = conv(cm_ref, cwc_ref, cbc_ref, rows, win, shifts)
        dt_raw = dt_ref[rows, :] + dtb_ref[...]
        dt = jnp.maximum(dt_raw, 0.0) + jnp.log(1.0 + jnp.exp(-jnp.abs(dt_raw)))
        da = dt * (-jnp.exp(alog_ref[...]))
        cs_col = jnp.dot(lower, da, precision=hi, preferred_element_type=F32)
        cs_row = jnp.dot(da.T, upper, precision=hi, preferred_element_type=F32)
        cs_last = cs_col[CHUNK - 1:CHUNK, :]
        to_end = jnp.exp(cs_last - cs_col)
        from_start = jnp.exp(cs_col)
        end_decay = jnp.exp(cs_last)
        dsk = dsk_ref[...]
        for g in range(SSM_G):
            bg = bm[:, g * SSM_N:(g + 1) * SSM_N]
            cg = cm[:, g * SSM_N:(g + 1) * SSM_N].astype(BF16)
            cb = lax.dot_general(cg, bg.astype(BF16), (((1,), (1,)), ((), ())),
                                 preferred_element_type=F32)
            bg_t = bg.T.astype(BF16)
            ssq = jnp.zeros((CHUNK, 1), F32)
            ys = []
            for pr in range(2):
                h0 = g * 4 + pr * 2
                blk = slice(h0 * SSM_P, (h0 + 2) * SSM_P)
                xh = xs[:, blk]
                xdt = xh * _pair(dt, h0, (CHUNK, LANES))
                xdt_b = xdt.astype(BF16)
                yd = []
                for hh in (h0, h0 + 1):
                    seg = cs_col[:, hh:hh + 1] - cs_row[hh:hh + 1, :]
                    lmat = jnp.exp(jnp.where(causal, seg, -jnp.inf))
                    yd.append(jnp.dot((cb * lmat).astype(BF16), xdt_b, preferred_element_type=F32))
                y_diag = jnp.where(first, yd[0], yd[1])
                state = st_ref[g, pr]
                y_off = jnp.dot(cg, state.astype(BF16), preferred_element_type=F32)
                y_off = y_off * _pair(from_start, h0, (CHUNK, LANES))
                upd = (xdt * _pair(to_end, h0, (CHUNK, LANES))).astype(BF16)
                st_ref[g, pr] = (state * _pair(end_decay, h0, (1, LANES))
                                 + jnp.dot(bg_t, upd, preferred_element_type=F32))
                y = y_diag + y_off + xh * _pair(dsk, h0, (1, LANES))
                y = y * _silu(z_ref[rows, blk].astype(F32))
                ssq = ssq + jnp.sum(y * y, axis=-1, keepdims=True)
                ys.append(y)
            inv = lax.rsqrt(ssq * (1.0 / (4 * SSM_P)) + EPS)
            for pr in range(2):
                h0 = g * 4 + pr * 2
                blk = slice(h0 * SSM_P, (h0 + 2) * SSM_P)
                o_ref[rows, blk] = (ys[pr] * inv * nw_ref[:, blk]).astype(BF16)
        return carry

    lax.fori_loop(0, n_chunks, step, 0)


def _ssd_call(proj, dt, conv_w, conv_b, dt_bias, a_log, d_skip, norm_w, *, batch, seq):
    t = proj.shape[0]
    gn = SSM_G * SSM_N
    pad = lambda v: jnp.pad(v, (0, LANES - SSM_HEADS)).reshape(1, LANES)
    conv_b2 = conv_b.reshape(1, -1)
    return pl.pallas_call(
        _ssd_body,
        grid=(batch,),
        in_specs=[pl.BlockSpec((seq, SSM_INNER), lambda b: (b, P_SZ // SSM_INNER)),
                  pl.BlockSpec((seq, SSM_INNER), lambda b: (b, P_SX // SSM_INNER)),
                  pl.BlockSpec((seq, gn), lambda b: (b, P_SB // gn)),
                  pl.BlockSpec((seq, gn), lambda b: (b, P_SC // gn)),
                  pl.BlockSpec((seq, LANES), lambda b: (b, 0)),
                  pl.BlockSpec((SSM_CONV, SSM_INNER), lambda b: (0, 0)),
                  pl.BlockSpec((SSM_CONV, gn), lambda b: (0, SSM_INNER // gn)),
                  pl.BlockSpec((SSM_CONV, gn), lambda b: (0, SSM_INNER // gn + 1)),
                  pl.BlockSpec((1, SSM_INNER), lambda b: (0, 0)),
                  pl.BlockSpec((1, gn), lambda b: (0, SSM_INNER // gn)),
                  pl.BlockSpec((1, gn), lambda b: (0, SSM_INNER // gn + 1)),
                  pl.BlockSpec((1, LANES), lambda b: (0, 0)),
                  pl.BlockSpec((1, LANES), lambda b: (0, 0)),
                  pl.BlockSpec((1, LANES), lambda b: (0, 0)),
                  pl.BlockSpec((1, SSM_INNER), lambda b: (0, 0))],
        out_specs=pl.BlockSpec((seq, SSM_INNER), lambda b: (b, 0)),
        out_shape=jax.ShapeDtypeStruct((t, SSM_INNER), BF16),
        scratch_shapes=[pltpu.VMEM((SSM_G, 2, SSM_N, LANES), F32)],
        compiler_params=_cparams(1),
        name="ssd",
    )(proj, proj, proj, proj, dt, conv_w, conv_w, conv_w, conv_b2, conv_b2, conv_b2,
      pad(dt_bias), pad(a_log), pad(d_skip), norm_w.reshape(1, -1))


def _band_body(q_ref, k_ref, v_ref, o_ref, lse_ref):
    n_q = q_ref.shape[0] // CHUNK
    first = lax.broadcasted_iota(jnp.int32, (CHUNK, LANES), 1) < ATT_E
    row1 = lax.broadcasted_iota(jnp.int32, (CHUNK, CHUNK), 0)
    col1 = lax.broadcasted_iota(jnp.int32, (CHUNK, CHUNK), 1)
    ok_cur = row1 >= col1
    dist2 = (lax.broadcasted_iota(jnp.int32, (CHUNK, 2 * CHUNK), 0)
             - lax.broadcasted_iota(jnp.int32, (CHUNK, 2 * CHUNK), 1) + CHUNK)
    ok_both = (dist2 >= 0) & (dist2 <= CHUNK)

    def q_block(q_rows, k_rows, ok):
        for pr in range(ATT_SLOTS // 2):
            cols = slice(pr * LANES, (pr + 1) * LANES)
            q = q_ref[q_rows, cols].astype(F32) * (ATT_E ** -0.5)
            k = k_ref[k_rows, cols]
            v = v_ref[k_rows, cols]
            outs, lses = [], []
            for sub in range(2):
                qm = (jnp.where(first, q, 0.0) if sub == 0 else jnp.where(first, 0.0, q)).astype(BF16)
                s = lax.dot_general(qm, k, (((1,), (1,)), ((), ())), preferred_element_type=F32)
                s = jnp.where(ok, s, -jnp.inf)
                m = jnp.max(s, axis=-1, keepdims=True)
                p = jnp.exp(s - m)
                l = jnp.sum(p, axis=-1, keepdims=True)
                pv = jnp.dot(p.astype(BF16), v, preferred_element_type=F32)
                outs.append(pv / l)
                lses.append(m + jnp.log(l))
            o_ref[q_rows, cols] = jnp.where(first, outs[0], outs[1]).astype(BF16)
            lse_ref[q_rows, cols] = jnp.where(first, lses[0], lses[1])

    q_block(pl.ds(0, CHUNK), pl.ds(0, CHUNK), ok_cur)
    if n_q > 1:
        def body(i, carry):
            q0 = pl.multiple_of(i * CHUNK, CHUNK)
            k0 = pl.multiple_of(q0 - CHUNK, CHUNK)
            q_block(pl.ds(q0, CHUNK), pl.ds(k0, 2 * CHUNK), ok_both)
            return carry

        lax.fori_loop(1, n_q, body, 0)


def _band_call(proj, gi, *, batch, seq):
    window, dil = ATT_PATTERNS[gi]
    assert window // dil == CHUNK and seq % (dil * CHUNK) == 0
    t, width = proj.shape
    ls = seq // dil
    w = ATT_SLOTS * ATT_E
    per_row = width // w
    view = proj.reshape(t // dil, dil * width)

    def spec(off):
        blk = (off + gi * w) // w
        return pl.BlockSpec((ls, w), lambda b, r: (b, r * per_row + blk))

    out_spec = pl.BlockSpec((ls, w), lambda b, r: (b, r))
    o, lse = pl.pallas_call(
        _band_body,
        grid=(batch, dil),
        in_specs=[spec(P_AQ), spec(P_AK), spec(P_AV)],
        out_specs=[out_spec, out_spec],
        out_shape=[jax.ShapeDtypeStruct((t // dil, dil * w), BF16),
                   jax.ShapeDtypeStruct((t // dil, dil * w), F32)],
        compiler_params=_cparams(2),
        name=f"band_attention_{gi}",
    )(view, view, view)
    return o.reshape(t, w), lse.reshape(t, w)


def _merge_body(x_ref, g0_ref, g1_ref, g2_ref, yr_ref, ys_ref, a0_ref, a1_ref, a2_ref,
                l0_ref, l1_ref, l2_ref, gate_ref, wr_ref, ws_ref, wa_ref, wo_ref, o_ref):
    lses = [r[...] for r in (l0_ref, l1_ref, l2_ref)]
    top = jnp.maximum(jnp.maximum(lses[0], lses[1]), lses[2])
    wts = [jnp.exp(l - top) for l in lses]
    num = sum(w * a[...].astype(F32) for w, a in zip(wts, (a0_ref, a1_ref, a2_ref)))
    y_att = (num / (wts[0] + wts[1] + wts[2])).astype(BF16)

    def branch(g_ref, y, w_ref):
        return jax.nn.sigmoid(g_ref[...].astype(F32)) * jnp.dot(
            y, w_ref[...], preferred_element_type=F32)

    mixed = (branch(g0_ref, yr_ref[...], wr_ref) + branch(g1_ref, ys_ref[...], ws_ref)
             + branch(g2_ref, y_att, wa_ref))
    upd = jnp.dot(mixed.astype(BF16), wo_ref[...], preferred_element_type=F32)
    o_ref[...] = x_ref[...] + gate_ref[0] * upd


def _merge_call(x2, proj, y_ret, y_ssm, att, ada3, wr, ws, wa, wo, *, seq, tm):
    t, d = x2.shape
    per_b = seq // tm
    const = lambda i: (0, 0)
    rows = lambda a: pl.BlockSpec((tm, a.shape[1]), lambda i: (i, 0))
    att_o = [o for o, _ in att]
    att_l = [l for _, l in att]
    return pl.pallas_call(
        _merge_body,
        grid=(t // tm,),
        in_specs=[pl.BlockSpec((tm, d), lambda i: (i, 0)),
                  pl.BlockSpec((tm, d), lambda i: (i, P_GATE // d)),
                  pl.BlockSpec((tm, d), lambda i: (i, P_GATE // d + 1)),
                  pl.BlockSpec((tm, d), lambda i: (i, P_GATE // d + 2)),
                  rows(y_ret), rows(y_ssm)]
                 + [rows(a) for a in att_o] + [rows(a) for a in att_l]
                 + [pl.BlockSpec((1, 1, d), lambda i: (i // per_b, 0, 2)),
                    pl.BlockSpec(wr.shape, const),
                    pl.BlockSpec(ws.shape, const),
                    pl.BlockSpec(wa.shape, const),
                    pl.BlockSpec(wo.shape, const)],
        out_specs=pl.BlockSpec((tm, d), lambda i: (i, 0)),
        out_shape=jax.ShapeDtypeStruct((t, d), F32),
        compiler_params=_cparams(1),
        name="merge_out",
    )(x2, proj, proj, proj, y_ret, y_ssm, *att_o, *att_l, ada3, wr, ws, wa, wo)


def _ffn_body(x_ref, nw_ref, sh_ref, sc_ref, gate_ref, w1_ref, w3_ref, w2_ref, fw_ref, o_ref,
              *, final_norm):
    x = x_ref[...]
    h = _rms(x) * nw_ref[...]
    h = (h * (1.0 + sc_ref[0]) + sh_ref[0]).astype(BF16)
    a = jnp.dot(h, w1_ref[...], preferred_element_type=F32)
    b = jnp.dot(h, w3_ref[...], preferred_element_type=F32)
    act = (_silu(a) * b).astype(BF16)
    y = x + gate_ref[0] * jnp.dot(act, w2_ref[...], preferred_element_type=F32)
    if final_norm:
        y = _rms(y) * fw_ref[...]
    o_ref[...] = y


def _ffn_call(x2, nw, ada3, w1, w3, w2, final_w, *, seq, tm, final_norm):
    t, d = x2.shape
    per_b = seq // tm
    const = lambda i: (0, 0)
    resident = dict(pipeline_mode=pl.Buffered(1))
    return pl.pallas_call(
        functools.partial(_ffn_body, final_norm=final_norm),
        grid=(t // tm,),
        in_specs=[pl.BlockSpec((tm, d), lambda i: (i, 0)),
                  pl.BlockSpec((1, d), const),
                  pl.BlockSpec((1, 1, d), lambda i: (i // per_b, 0, 3)),
                  pl.BlockSpec((1, 1, d), lambda i: (i // per_b, 0, 4)),
                  pl.BlockSpec((1, 1, d), lambda i: (i // per_b, 0, 5)),
                  pl.BlockSpec(w1.shape, const, **resident),
                  pl.BlockSpec(w3.shape, const, **resident),
                  pl.BlockSpec(w2.shape, const, **resident),
                  pl.BlockSpec((1, d), const)],
        out_specs=pl.BlockSpec((tm, d), lambda i: (i, 0)),
        out_shape=jax.ShapeDtypeStruct((t, d), F32),
        compiler_params=_cparams(1),
        name="ffn",
    )(x2, nw, ada3, ada3, ada3, w1, w3, w2, final_w)


def _tile(seq, want):
    return min(seq, want)


def kernel(x, c, ada_w, ada_b, norm1_w, w_in, ssm_conv_w, ssm_conv_b, ssm_dt_bias, ssm_a_log,
           ssm_d, ssm_norm_w, w_br_ret, w_br_ssm, w_br_att, w_out, norm2_w, w_ff1, w_ff3, w_ff2,
           final_norm_w):
    batch, seq, d = x.shape
    depth = ada_w.shape[0]
    assert d == D_MODEL and seq % (2 * CHUNK) == 0 and w_in.shape[2] == _O_END
    t = batch * seq
    ada = _ada_call(c, ada_w.astype(BF16), ada_b)
    tables = _ret_tables(seq)
    x2 = x.reshape(t, d)
    final_w = final_norm_w.reshape(1, d)
    for i in range(depth):
        ada3 = ada[i].reshape(batch, 1, 6 * d)
        w = w_in[i]
        w_main = jnp.concatenate([w[:, _O_GATE:_O_END], w[:, :_O_DT], w[:, _O_AQ:_O_GATE]],
                                 axis=1).astype(BF16)
        w_dt = jnp.pad(w[:, _O_DT:_O_AQ], ((0, 0), (0, LANES - SSM_HEADS))).astype(BF16)
        proj, dt = _in_call(x2, norm1_w[i].reshape(1, d), ada3, w_main, w_dt,
                            seq=seq, tm=_tile(seq, 1024), tn=1280)
        y_ret = _ret_call(proj, tables, batch=batch, seq=seq)
        y_ssm = _ssd_call(proj, dt, ssm_conv_w[i], ssm_conv_b[i], ssm_dt_bias[i], ssm_a_log[i],
                          ssm_d[i], ssm_norm_w[i], batch=batch, seq=seq)
        att = [_band_call(proj, gi, batch=batch, seq=seq) for gi in range(len(ATT_PATTERNS))]
        x2 = _merge_call(x2, proj, y_ret, y_ssm, att, ada3, w_br_ret[i].astype(BF16),
                         w_br_ssm[i].astype(BF16), w_br_att[i].astype(BF16),
                         w_out[i].astype(BF16), seq=seq, tm=_tile(seq, 512))
        x2 = _ffn_call(x2, norm2_w[i].reshape(1, d), ada3, w_ff1[i].astype(BF16),
                       w_ff3[i].astype(BF16), w_ff2[i].astype(BF16), final_w,
                       seq=seq, tm=_tile(seq, 512), final_norm=(i == depth - 1))
    return x2.reshape(batch, seq, d)
```

```python
import functools
import math

import numpy as np
import jax
import jax.numpy as jnp
from jax import lax
from jax.experimental import pallas as pl
from jax.experimental.pallas import tpu as pltpu

F32 = jnp.float32
BF16 = jnp.bfloat16

D_MODEL = 1024
EPS = 1e-6
CHUNK = 128
RET_HEADS = 4
RET_QK = 128
RET_V = 256
ROPE_BASE = 10000.0
SSM_HEADS = 16
SSM_P = 64
SSM_G = 4
SSM_N = 128
SSM_CONV = 4
SSM_INNER = SSM_HEADS * SSM_P
ATT_PATTERNS = ((128, 1), (512, 4), (2048, 16))
ATT_SLOTS = 4
ATT_E = 64
ATT_W = len(ATT_PATTERNS) * ATT_SLOTS * ATT_E
N_BRANCHES = 3

_O_RQ = 0
_O_SXBC = 4096
_O_DT = 6144
_O_AQ = 6160
_O_GATE = 8464
_O_END = 11536
P_GATE = 0
P_RQ = 3072
P_RK = 3584
P_RV = 4096
P_RG = 5120
P_SZ = 6144
P_SX = 7168
P_SB = 8192
P_SC = 8704
P_AQ = 9216
P_AK = 9984
P_AV = 10752
P_WIDTH = 11520

LANES = 128
VMEM_LIMIT = 56 * 1024 * 1024


def _cparams(n_axes):
    return pltpu.CompilerParams(
        dimension_semantics=("arbitrary",) * n_axes, vmem_limit_bytes=VMEM_LIMIT)


def _silu(v):
    return v * jax.nn.sigmoid(v)


def _rms(v):
    return v * lax.rsqrt(jnp.mean(v * v, axis=-1, keepdims=True) + EPS)


def _ada_body(c_ref, w_ref, b_ref, o_ref):
    s = _silu(c_ref[...])
    o_ref[0] = jnp.dot(s.astype(BF16), w_ref[0], preferred_element_type=F32) + b_ref[0]


def _ada_call(c, ada_w, ada_b):
    depth, d, n = ada_w.shape
    b = c.shape[0]
    return pl.pallas_call(
        _ada_body,
        grid=(depth, n // d),
        in_specs=[pl.BlockSpec((b, d), lambda l, j: (0, 0)),
                  pl.BlockSpec((1, d, d), lambda l, j: (l, 0, j)),
                  pl.BlockSpec((1, 1, d), lambda l, j: (l, 0, j))],
        out_specs=pl.BlockSpec((1, b, d), lambda l, j: (l, 0, j)),
        out_shape=jax.ShapeDtypeStruct((depth, b, n), F32),
        compiler_params=_cparams(2),
        name="ada",
    )(c, ada_w, ada_b.reshape(depth, 1, n))


def _in_body(x_ref, nw_ref, sh_ref, sc_ref, w_ref, wdt_ref, o_ref, dt_ref, h_ref):
    @pl.when(pl.program_id(1) == 0)
    def _():
        h = _rms(x_ref[...]) * nw_ref[...]
        h = (h * (1.0 + sc_ref[0]) + sh_ref[0]).astype(BF16)
        h_ref[...] = h
        dt_ref[...] = jnp.dot(h, wdt_ref[...], preferred_element_type=F32)

    o_ref[...] = jnp.dot(h_ref[...], w_ref[...], preferred_element_type=F32).astype(BF16)


def _in_call(x2, nw, ada3, w_main, w_dt, *, seq, tm, tn):
    t, d = x2.shape
    n = w_main.shape[1]
    per_b = seq // tm
    return pl.pallas_call(
        _in_body,
        grid=(t // tm, n // tn),
        in_specs=[pl.BlockSpec((tm, d), lambda i, j: (i, 0)),
                  pl.BlockSpec((1, d), lambda i, j: (0, 0)),
                  pl.BlockSpec((1, 1, d), lambda i, j: (i // per_b, 0, 0)),
                  pl.BlockSpec((1, 1, d), lambda i, j: (i // per_b, 0, 1)),
                  pl.BlockSpec((d, tn), lambda i, j: (0, j)),
                  pl.BlockSpec((d, LANES), lambda i, j: (0, 0))],
        out_specs=[pl.BlockSpec((tm, tn), lambda i, j: (i, j)),
                   pl.BlockSpec((tm, LANES), lambda i, j: (i, 0))],
        out_shape=[jax.ShapeDtypeStruct((t, n), BF16),
                   jax.ShapeDtypeStruct((t, LANES), F32)],
        scratch_shapes=[pltpu.VMEM((tm, d), BF16)],
        compiler_params=_cparams(2),
        name="in_proj",
    )(x2, nw, ada3, ada3, w_main, w_dt)


def _ret_tables(seq):
    half = RET_QK // 2
    inv_freq = ROPE_BASE ** (-np.arange(half, dtype=np.float64) / half)
    ang = np.arange(seq, dtype=np.float64)[:, None] * inv_freq[None, :]
    cos, sin = np.cos(ang), np.sin(ang)
    cos2 = np.concatenate([cos, cos], axis=1)
    sin2 = np.concatenate([-sin, sin], axis=1)
    lg = np.log1p(-np.exp(np.linspace(math.log(1.0 / 32), math.log(1.0 / 512), RET_HEADS)))
    idx = np.arange(CHUNK, dtype=np.float64)
    diff = idx[:, None] - idx[None, :]
    intra = np.where(diff[None] >= 0, np.exp(diff[None] * lg[:, None, None]), 0.0)
    cross_q = np.exp((idx[None, :, None] + 1.0) * lg[:, None, None])
    state_k = np.exp((CHUNK - 1.0 - idx[None, :, None]) * lg[:, None, None])
    chunk_decay = np.exp(CHUNK * lg)[:, None, None]
    as32 = lambda a: jnp.asarray(a, F32)
    return (as32(cos2), as32(sin2), as32(intra),
            as32(np.broadcast_to(cross_q, (RET_HEADS, CHUNK, RET_V))),
            as32(np.broadcast_to(state_k, (RET_HEADS, CHUNK, RET_QK))),
            as32(np.broadcast_to(chunk_decay, (RET_HEADS, 1, RET_V))))


def _ret_body(q_ref, k_ref, v_ref, g_ref, cos_ref, sin_ref, intra_ref, cq_ref, sk_ref, cd_ref,
              o_ref, r_ref):
    n_chunks = q_ref.shape[0] // CHUNK
    r_ref[...] = jnp.zeros_like(r_ref)

    def step(c, carry):
        rows = pl.ds(pl.multiple_of(c * CHUNK, CHUNK), CHUNK)
        cos = cos_ref[rows, :]
        sin = sin_ref[rows, :]
        for h in range(RET_HEADS):
            qk_cols = slice(h * RET_QK, (h + 1) * RET_QK)
            v_cols = slice(h * RET_V, (h + 1) * RET_V)
            q = q_ref[rows, qk_cols].astype(F32)
            k = k_ref[rows, qk_cols].astype(F32)
            qr = q * cos + pltpu.roll(q, RET_QK // 2, 1) * sin
            kr = (k * cos + pltpu.roll(k, RET_QK // 2, 1) * sin) * (RET_QK ** -0.5)
            qb = qr.astype(BF16)
            v = v_ref[rows, v_cols]
            scores = lax.dot_general(qb, kr.astype(BF16), (((1,), (1,)), ((), ())),
                                     preferred_element_type=F32) * intra_ref[h]
            inner = jnp.dot(scores.astype(BF16), v, preferred_element_type=F32)
            state = r_ref[h]
            cross = jnp.dot(qb, state.astype(BF16), preferred_element_type=F32) * cq_ref[h]
            ks_t = (kr * sk_ref[h]).T.astype(BF16)
            r_ref[h] = state * cd_ref[h] + jnp.dot(ks_t, v, preferred_element_type=F32)
            y = inner + cross
            dlt = y - jnp.mean(y, axis=-1, keepdims=True)
            yn = dlt * lax.rsqrt(jnp.mean(dlt * dlt, axis=-1, keepdims=True) + EPS)
            o_ref[rows, v_cols] = (_silu(g_ref[rows, v_cols].astype(F32)) * yn).astype(BF16)
        return carry

    lax.fori_loop(0, n_chunks, step, 0)


def _ret_call(proj, tables, *, batch, seq):
    cos2, sin2, intra, cq, sk, cd = tables
    t = proj.shape[0]
    wqk, wv = RET_HEADS * RET_QK, RET_HEADS * RET_V
    whole = lambda a: pl.BlockSpec(a.shape, lambda b: (0,) * a.ndim)
    return pl.pallas_call(
        _ret_body,
        grid=(batch,),
        in_specs=[pl.BlockSpec((seq, wqk), lambda b: (b, P_RQ // wqk)),
                  pl.BlockSpec((seq, wqk), lambda b: (b, P_RK // wqk)),
                  pl.BlockSpec((seq, wv), lambda b: (b, P_RV // wv)),
                  pl.BlockSpec((seq, wv), lambda b: (b, P_RG // wv)),
                  whole(cos2), whole(sin2), whole(intra), whole(cq), whole(sk), whole(cd)],
        out_specs=pl.BlockSpec((seq, wv), lambda b: (b, 0)),
        out_shape=jax.ShapeDtypeStruct((t, wv), BF16),
        scratch_shapes=[pltpu.VMEM((RET_HEADS, RET_QK, RET_V), F32)],
        compiler_params=_cparams(1),
        name="retention",
    )(proj, proj, proj, proj, cos2, sin2, intra, cq, sk, cd)


def _pair(col_src, h0, shape):
    lane = lax.broadcasted_iota(jnp.int32, shape, 1)
    a = jnp.broadcast_to(col_src[:, h0:h0 + 1], shape)
    b = jnp.broadcast_to(col_src[:, h0 + 1:h0 + 2], shape)
    return jnp.where(lane < SSM_P, a, b)


def _ssd_body(z_ref, xs_ref, bm_ref, cm_ref, dt_ref, cwx_ref, cwb_ref, cwc_ref, cbx_ref, cbb_ref,
              cbc_ref, dtb_ref, alog_ref, dsk_ref, nw_ref, o_ref, st_ref):
    n_chunks = z_ref.shape[0] // CHUNK
    st_ref[...] = jnp.zeros_like(st_ref)
    row = lax.broadcasted_iota(jnp.int32, (CHUNK, CHUNK), 0)
    col = lax.broadcasted_iota(jnp.int32, (CHUNK, CHUNK), 1)
    causal = row >= col
    lower = causal.astype(F32)
    upper = (row <= col).astype(F32)
    lane = lax.broadcasted_iota(jnp.int32, (CHUNK, LANES), 1)
    first = lane < SSM_P
    hi = lax.Precision.HIGHEST

    wrow = lax.broadcasted_iota(jnp.int32, (CHUNK, 2 * CHUNK), 0)
    wcol = lax.broadcasted_iota(jnp.int32, (CHUNK, 2 * CHUNK), 1)

    def conv(x_ref, w_ref, b_ref, rows, win, shifts):
        window = x_ref[win, :]
        acc = b_ref[...] + x_ref[rows, :].astype(F32) * w_ref[SSM_CONV - 1:SSM_CONV, :]
        for k in range(SSM_CONV - 1):
            acc = acc + jnp.dot(shifts[k], window, preferred_element_type=F32) * w_ref[k:k + 1, :]
        return _silu(acc)

    def step(c, carry):
        rows = pl.ds(pl.multiple_of(c * CHUNK, CHUNK), CHUNK)
        first_chunk = c == 0
        win = pl.ds(pl.multiple_of(jnp.maximum(c - 1, 0) * CHUNK, CHUNK), 2 * CHUNK)
        cur_off = jnp.where(first_chunk, 0, CHUNK)
        shifts = [jnp.where(wcol == wrow + (cur_off - (SSM_CONV - 1 - k)), 1.0, 0.0).astype(BF16)
                  for k in range(SSM_CONV - 1)]
        xs = conv(xs_ref, cwx_ref, cbx_ref, rows, win, shifts)
        bm = conv(bm_ref, cwb_ref, cbb_ref, rows, win, shifts)
        cm = conv(cm_ref, cwc_ref, cbc_ref, rows, win, shifts)
        dt_raw = dt_ref[rows, :] + dtb_ref[...]
        dt = jnp.maximum(dt_raw, 0.0) + jnp.log(1.0 + jnp.exp(-jnp.abs(dt_raw)))
        da = dt * (-jnp.exp(alog_ref[...]))
        cs_col = jnp.dot(lower, da, precision=hi, preferred_element_type=F32)
        cs_row = jnp.dot(da.T, upper, precision=hi, preferred_element_type=F32)
        cs_last = cs_col[CHUNK - 1:CHUNK, :]
        to_end = jnp.exp(cs_last - cs_col)
        from_start = jnp.exp(cs_col)
        end_decay = jnp.exp(cs_last)
        dsk = dsk_ref[...]
        for g in range(SSM_G):
            bg = bm[:, g * SSM_N:(g + 1) * SSM_N]
            cg = cm[:, g * SSM_N:(g + 1) * SSM_N].astype(BF16)
            cb = lax.dot_general(cg, bg.astype(BF16), (((1,), (1,)), ((), ())),
                                 preferred_element_type=F32)
            bg_t = bg.T.astype(BF16)
            ssq = jnp.zeros((CHUNK, 1), F32)
            ys = []
            for pr in range(2):
                h0 = g * 4 + pr * 2
                blk = slice(h0 * SSM_P, (h0 + 2) * SSM_P)
                xh = xs[:, blk]
                xdt = xh * _pair(dt, h0, (CHUNK, LANES))
                xdt_b = xdt.astype(BF16)
                yd = []
                for hh in (h0, h0 + 1):
                    seg = cs_col[:, hh:hh + 1] - cs_row[hh:hh + 1, :]
                    lmat = jnp.exp(jnp.where(causal, seg, -jnp.inf))
                    yd.append(jnp.dot((cb * lmat).astype(BF16), xdt_b, preferred_element_type=F32))
                y_diag = jnp.where(first, yd[0], yd[1])
                state = st_ref[g, pr]
                y_off = jnp.dot(cg, state.astype(BF16), preferred_element_type=F32)
                y_off = y_off * _pair(from_start, h0, (CHUNK, LANES))
                upd = (xdt * _pair(to_end, h0, (CHUNK, LANES))).astype(BF16)
                st_ref[g, pr] = (state * _pair(end_decay, h0, (1, LANES))
                                 + jnp.dot(bg_t, upd, preferred_element_type=F32))
                y = y_diag + y_off + xh * _pair(dsk, h0, (1, LANES))
                y = y * _silu(z_ref[rows, blk].astype(F32))
                ssq = ssq + jnp.sum(y * y, axis=-1, keepdims=True)
                ys.append(y)
            inv = lax.rsqrt(ssq * (1.0 / (4 * SSM_P)) + EPS)
            for pr in range(2):
                h0 = g * 4 + pr * 2
                blk = slice(h0 * SSM_P, (h0 + 2) * SSM_P)
                o_ref[rows, blk] = (ys[pr] * inv * nw_ref[:, blk]).astype(BF16)
        return carry

    lax.fori_loop(0, n_chunks, step, 0)


def _ssd_call(proj, dt, conv_w, conv_b, dt_bias, a_log, d_skip, norm_w, *, batch, seq):
    t = proj.shape[0]
    gn = SSM_G * SSM_N
    pad = lambda v: jnp.pad(v, (0, LANES - SSM_HEADS)).reshape(1, LANES)
    conv_b2 = conv_b.reshape(1, -1)
    return pl.pallas_call(
        _ssd_body,
        grid=(batch,),
        in_specs=[pl.BlockSpec((seq, SSM_INNER), lambda b: (b, P_SZ // SSM_INNER)),
                  pl.BlockSpec((seq, SSM_INNER), lambda b: (b, P_SX // SSM_INNER)),
                  pl.BlockSpec((seq, gn), lambda b: (b, P_SB // gn)),
                  pl.BlockSpec((seq, gn), lambda b: (b, P_SC // gn)),
                  pl.BlockSpec((seq, LANES), lambda b: (b, 0)),
                  pl.BlockSpec((SSM_CONV, SSM_INNER), lambda b: (0, 0)),
                  pl.BlockSpec((SSM_CONV, gn), lambda b: (0, SSM_INNER // gn)),
                  pl.BlockSpec((SSM_CONV, gn), lambda b: (0, SSM_INNER // gn + 1)),
                  pl.BlockSpec((1, SSM_INNER), lambda b: (0, 0)),
                  pl.BlockSpec((1, gn), lambda b: (0, SSM_INNER // gn)),
                  pl.BlockSpec((1, gn), lambda b: (0, SSM_INNER // gn + 1)),
                  pl.BlockSpec((1, LANES), lambda b: (0, 0)),
                  pl.BlockSpec((1, LANES), lambda b: (0, 0)),
                  pl.BlockSpec((1, LANES), lambda b: (0, 0)),
                  pl.BlockSpec((1, SSM_INNER), lambda b: (0, 0))],
        out_specs=pl.BlockSpec((seq, SSM_INNER), lambda b: (b, 0)),
        out_shape=jax.ShapeDtypeStruct((t, SSM_INNER), BF16),
        scratch_shapes=[pltpu.VMEM((SSM_G, 2, SSM_N, LANES), F32)],
        compiler_params=_cparams(1),
        name="ssd",
    )(proj, proj, proj, proj, dt, conv_w, conv_w, conv_w, conv_b2, conv_b2, conv_b2,
      pad(dt_bias), pad(a_log), pad(d_skip), norm_w.reshape(1, -1))


def _att_proj_body(x_ref, nw_ref, sh_ref, sc_ref, w_ref, o_ref):
    h = _rms(x_ref[...]) * nw_ref[...]
    h = (h * (1.0 + sc_ref[0]) + sh_ref[0]).astype(BF16)
    o_ref[...] = jnp.dot(h, w_ref[...], preferred_element_type=F32)


def _att_proj_call(x2, nw, ada3, w_att, *, seq, tm):
    t, d = x2.shape
    n = w_att.shape[1]
    per_b = seq // tm
    return pl.pallas_call(
        _att_proj_body,
        grid=(t // tm,),
        in_specs=[pl.BlockSpec((tm, d), lambda i: (i, 0)),
                  pl.BlockSpec((1, d), lambda i: (0, 0)),
                  pl.BlockSpec((1, 1, d), lambda i: (i // per_b, 0, 0)),
                  pl.BlockSpec((1, 1, d), lambda i: (i // per_b, 0, 1)),
                  pl.BlockSpec((d, n), lambda i: (0, 0))],
        out_specs=pl.BlockSpec((tm, n), lambda i: (i, 0)),
        out_shape=jax.ShapeDtypeStruct((t, n), F32),
        compiler_params=_cparams(1),
        name="att_proj",
    )(x2, nw, ada3, ada3, w_att)


def _band_body(q0_ref, q1_ref, k0_ref, k1_ref, v0_ref, v1_ref, o_ref, lse_ref, *, dil):
    seq = q0_ref.shape[0]
    n_q = seq // dil // CHUNK
    qkv = ((q0_ref, k0_ref, v0_ref), (q1_ref, k1_ref, v1_ref))
    first = lax.broadcasted_iota(jnp.int32, (CHUNK, LANES), 1) < ATT_E

    def band_mask(nk):
        pos = lax.broadcasted_iota(jnp.int32, (2 * CHUNK, nk), 0) & (CHUNK - 1)
        dist = pos - lax.broadcasted_iota(jnp.int32, (2 * CHUNK, nk), 1) + (nk - CHUNK)
        return (dist >= 0) & (dist <= CHUNK)

    ok_cur, ok_both = band_mask(CHUNK), band_mask(2 * CHUNK)

    def rows(r, p0, n):
        return pl.ds(r + dil * p0, n, stride=dil) if dil > 1 else pl.ds(r + p0, n)

    for r in range(dil):
        for j in range(n_q):
            q_rows = rows(r, j * CHUNK, CHUNK)
            k_rows = rows(r, (j - 1) * CHUNK, 2 * CHUNK) if j else q_rows
            ok = ok_both if j else ok_cur
            for pr, (q_ref, k_ref, v_ref) in enumerate(qkv):
                q = q_ref[q_rows, :] * (ATT_E ** -0.5)
                qs = jnp.concatenate([jnp.where(first, q, 0.0), jnp.where(first, 0.0, q)],
                                     axis=0).astype(BF16)
                k = k_ref[k_rows, :].astype(BF16)
                v = v_ref[k_rows, :].astype(BF16)
                s = lax.dot_general(qs, k, (((1,), (1,)), ((), ())), preferred_element_type=F32)
                s = jnp.where(ok, s, -jnp.inf)
                m = jnp.max(s, axis=-1, keepdims=True)
                p = jnp.exp(s - m)
                l = jnp.sum(p, axis=-1, keepdims=True)
                out = jnp.dot(p.astype(BF16), v, preferred_element_type=F32) / l
                lse = m + jnp.log(l)
                o_ref[pr, q_rows, :] = jnp.where(first, out[:CHUNK], out[CHUNK:])
                lse_ref[pr, q_rows, :] = jnp.where(first, lse[:CHUNK], lse[CHUNK:])


def _band_call(att, gi, *, batch, seq):
    window, dil = ATT_PATTERNS[gi]
    assert window // dil == CHUNK and seq % (dil * CHUNK) == 0
    t = att.shape[0]
    pairs = ATT_SLOTS // 2
    col0 = gi * 3 * pairs
    in_specs = [pl.BlockSpec((seq, LANES), lambda b, c=col0 + part * pairs + pr: (b, c))
                for pr in range(pairs) for part in range(3)]
    order = [0, 3, 1, 4, 2, 5]
    out_spec = pl.BlockSpec((pairs, seq, LANES), lambda b: (0, b, 0))
    out_shape = jax.ShapeDtypeStruct((pairs, t, LANES), F32)
    return pl.pallas_call(
        functools.partial(_band_body, dil=dil),
        grid=(batch,),
        in_specs=[in_specs[i] for i in order],
        out_specs=[out_spec, out_spec],
        out_shape=[out_shape, out_shape],
        compiler_params=_cparams(1),
        name=f"band_attention_{gi}",
    )(*([att] * 6))


def _merge_body(x_ref, g0_ref, g1_ref, g2_ref, yr_ref, ys_ref, a0_ref, a1_ref, a2_ref,
                l0_ref, l1_ref, l2_ref, gate_ref, wr_ref, ws_ref, wa_ref, wo_ref, o_ref):
    att_proj = None
    for pr in range(ATT_SLOTS // 2):
        lses = [r[pr] for r in (l0_ref, l1_ref, l2_ref)]
        top = jnp.maximum(jnp.maximum(lses[0], lses[1]), lses[2])
        wts = [jnp.exp(l - top) for l in lses]
        num = sum(w * a[pr] for w, a in zip(wts, (a0_ref, a1_ref, a2_ref)))
        y_att = (num / (wts[0] + wts[1] + wts[2])).astype(BF16)
        part = jnp.dot(y_att, wa_ref[pr * LANES:(pr + 1) * LANES, :], preferred_element_type=F32)
        att_proj = part if att_proj is None else att_proj + part

    def branch(g_ref, y, w_ref):
        return jax.nn.sigmoid(g_ref[...].astype(F32)) * jnp.dot(
            y, w_ref[...], preferred_element_type=F32)

    mixed = (branch(g0_ref, yr_ref[...], wr_ref) + branch(g1_ref, ys_ref[...], ws_ref)
             + jax.nn.sigmoid(g2_ref[...].astype(F32)) * att_proj)
    upd = jnp.dot(mixed.astype(BF16), wo_ref[...], preferred_element_type=F32)
    o_ref[...] = x_ref[...] + gate_ref[0] * upd


def _merge_call(x2, proj, y_ret, y_ssm, att, ada3, wr, ws, wa, wo, *, seq, tm):
    t, d = x2.shape
    per_b = seq // tm
    const = lambda i: (0, 0)
    rows = lambda a: pl.BlockSpec((tm, a.shape[1]), lambda i: (i, 0))
    pair_rows = lambda a: pl.BlockSpec((a.shape[0], tm, a.shape[2]), lambda i: (0, i, 0))
    att_o = [o for o, _ in att]
    att_l = [l for _, l in att]
    return pl.pallas_call(
        _merge_body,
        grid=(t // tm,),
        in_specs=[pl.BlockSpec((tm, d), lambda i: (i, 0)),
                  pl.BlockSpec((tm, d), lambda i: (i, P_GATE // d)),
                  pl.BlockSpec((tm, d), lambda i: (i, P_GATE // d + 1)),
                  pl.BlockSpec((tm, d), lambda i: (i, P_GATE // d + 2)),
                  rows(y_ret), rows(y_ssm)]
                 + [pair_rows(a) for a in att_o] + [pair_rows(a) for a in att_l]
                 + [pl.BlockSpec((1, 1, d), lambda i: (i // per_b, 0, 2)),
                    pl.BlockSpec(wr.shape, const),
                    pl.BlockSpec(ws.shape, const),
                    pl.BlockSpec(wa.shape, const),
                    pl.BlockSpec(wo.shape, const)],
        out_specs=pl.BlockSpec((tm, d), lambda i: (i, 0)),
        out_shape=jax.ShapeDtypeStruct((t, d), F32),
        compiler_params=_cparams(1),
        name="merge_out",
    )(x2, proj, proj, proj, y_ret, y_ssm, *att_o, *att_l, ada3, wr, ws, wa, wo)


def _ffn_body(x_ref, nw_ref, sh_ref, sc_ref, gate_ref, w1_ref, w3_ref, w2_ref, fw_ref, o_ref,
              *, final_norm):
    x = x_ref[...]
    h = _rms(x) * nw_ref[...]
    h = (h * (1.0 + sc_ref[0]) + sh_ref[0]).astype(BF16)
    a = jnp.dot(h, w1_ref[...], preferred_element_type=F32)
    b = jnp.dot(h, w3_ref[...], preferred_element_type=F32)
    act = (_silu(a) * b).astype(BF16)
    y = x + gate_ref[0] * jnp.dot(act, w2_ref[...], preferred_element_type=F32)
    if final_norm:
        y = _rms(y) * fw_ref[...]
    o_ref[...] = y


def _ffn_call(x2, nw, ada3, w1, w3, w2, final_w, *, seq, tm, final_norm):
    t, d = x2.shape
    per_b = seq // tm
    const = lambda i: (0, 0)
    resident = dict(pipeline_mode=pl.Buffered(1))
    return pl.pallas_call(
        functools.partial(_ffn_body, final_norm=final_norm),
        grid=(t // tm,),
        in_specs=[pl.BlockSpec((tm, d), lambda i: (i, 0)),
                  pl.BlockSpec((1, d), const),
                  pl.BlockSpec((1, 1, d), lambda i: (i // per_b, 0, 3)),
                  pl.BlockSpec((1, 1, d), lambda i: (i // per_b, 0, 4)),
                  pl.BlockSpec((1, 1, d), lambda i: (i // per_b, 0, 5)),
                  pl.BlockSpec(w1.shape, const, **resident),
                  pl.BlockSpec(w3.shape, const, **resident),
                  pl.BlockSpec(w2.shape, const, **resident),
                  pl.BlockSpec((1, d), const)],
        out_specs=pl.BlockSpec((tm, d), lambda i: (i, 0)),
        out_shape=jax.ShapeDtypeStruct((t, d), F32),
        compiler_params=_cparams(1),
        name="ffn",
    )(x2, nw, ada3, ada3, ada3, w1, w3, w2, final_w)


def _tile(seq, want):
    return min(seq, want)


def kernel(x, c, ada_w, ada_b, norm1_w, w_in, ssm_conv_w, ssm_conv_b, ssm_dt_bias, ssm_a_log,
           ssm_d, ssm_norm_w, w_br_ret, w_br_ssm, w_br_att, w_out, norm2_w, w_ff1, w_ff3, w_ff2,
           final_norm_w):
    batch, seq, d = x.shape
    depth = ada_w.shape[0]
    assert d == D_MODEL and seq % (2 * CHUNK) == 0 and w_in.shape[2] == _O_END
    t = batch * seq
    ada = _ada_call(c, ada_w.astype(BF16), ada_b)
    tables = _ret_tables(seq)
    x2 = x.reshape(t, d)
    final_w = final_norm_w.reshape(1, d)
    for i in range(depth):
        ada3 = ada[i].reshape(batch, 1, 6 * d)
        w = w_in[i]
        w_main = jnp.concatenate([w[:, _O_GATE:_O_END], w[:, :_O_DT]], axis=1).astype(BF16)
        w_dt = jnp.pad(w[:, _O_DT:_O_AQ], ((0, 0), (0, LANES - SSM_HEADS))).astype(BF16)
        gw = ATT_SLOTS * ATT_E
        w_att = jnp.concatenate(
            [w[:, _O_AQ + part * ATT_W + g * gw:_O_AQ + part * ATT_W + (g + 1) * gw]
             for g in range(len(ATT_PATTERNS)) for part in range(3)], axis=1).astype(BF16)
        nw1 = norm1_w[i].reshape(1, d)
        proj, dt = _in_call(x2, nw1, ada3, w_main, w_dt, seq=seq, tm=_tile(seq, 1024), tn=1536)
        qkv = _att_proj_call(x2, nw1, ada3, w_att, seq=seq, tm=_tile(seq, 512))
        y_ret = _ret_call(proj, tables, batch=batch, seq=seq)
        y_ssm = _ssd_call(proj, dt, ssm_conv_w[i], ssm_conv_b[i], ssm_dt_bias[i], ssm_a_log[i],
                          ssm_d[i], ssm_norm_w[i], batch=batch, seq=seq)
        att = [_band_call(qkv, gi, batch=batch, seq=seq) for gi in range(len(ATT_PATTERNS))]
        x2 = _merge_call(x2, proj, y_ret, y_ssm, att, ada3, w_br_ret[i].astype(BF16),
                         w_br_ssm[i].astype(BF16), w_br_att[i].astype(BF16),
                         w_out[i].astype(BF16), seq=seq, tm=_tile(seq, 512))
        x2 = _ffn_call(x2, norm2_w[i].reshape(1, d), ada3, w_ff1[i].astype(BF16),
                       w_ff3[i].astype(BF16), w_ff2[i].astype(BF16), final_w,
                       seq=seq, tm=_tile(seq, 512), final_norm=(i == depth - 1))
    return x2.reshape(batch, seq, d)
```

```python
import functools
import math

import numpy as np
import jax
import jax.numpy as jnp
from jax import lax
from jax.experimental import pallas as pl
from jax.experimental.pallas import tpu as pltpu

F32 = jnp.float32
BF16 = jnp.bfloat16

D_MODEL = 1024
EPS = 1e-6
CHUNK = 128
RET_HEADS = 4
RET_QK = 128
RET_V = 256
ROPE_BASE = 10000.0
SSM_HEADS = 16
SSM_P = 64
SSM_G = 4
SSM_N = 128
SSM_CONV = 4
SSM_INNER = SSM_HEADS * SSM_P
ATT_PATTERNS = ((128, 1), (512, 4), (2048, 16))
ATT_SLOTS = 4
ATT_E = 64
ATT_W = len(ATT_PATTERNS) * ATT_SLOTS * ATT_E
N_BRANCHES = 3

_O_RQ = 0
_O_SXBC = 4096
_O_DT = 6144
_O_AQ = 6160
_O_GATE = 8464
_O_END = 11536
P_GATE = 0
P_RQ = 3072
P_RK = 3584
P_RV = 4096
P_RG = 5120
P_SZ = 6144
P_SX = 7168
P_SB = 8192
P_SC = 8704

LANES = 128
VMEM_LIMIT = 56 * 1024 * 1024


def _cparams(n_axes):
    return pltpu.CompilerParams(
        dimension_semantics=("arbitrary",) * n_axes, vmem_limit_bytes=VMEM_LIMIT)


def _silu(v):
    return v * jax.nn.sigmoid(v)


def _row_sums(v):
    hi = v.astype(BF16)
    lo = (v - hi.astype(F32)).astype(BF16)
    both = jnp.concatenate([hi, lo], axis=1)
    return jnp.dot(both, jnp.ones((both.shape[1], LANES), BF16), preferred_element_type=F32)


def _rms(v):
    return v * lax.rsqrt(jnp.mean(v * v, axis=-1, keepdims=True) + EPS)


def _ada_body(c_ref, w_ref, b_ref, o_ref):
    s = _silu(c_ref[...])
    o_ref[0] = jnp.dot(s.astype(BF16), w_ref[0], preferred_element_type=F32) + b_ref[0]


def _ada_call(c, ada_w, ada_b):
    depth, d, n = ada_w.shape
    b = c.shape[0]
    return pl.pallas_call(
        _ada_body,
        grid=(depth, n // d),
        in_specs=[pl.BlockSpec((b, d), lambda l, j: (0, 0)),
                  pl.BlockSpec((1, d, d), lambda l, j: (l, 0, j)),
                  pl.BlockSpec((1, 1, d), lambda l, j: (l, 0, j))],
        out_specs=pl.BlockSpec((1, b, d), lambda l, j: (l, 0, j)),
        out_shape=jax.ShapeDtypeStruct((depth, b, n), F32),
        compiler_params=_cparams(2),
        name="ada",
    )(c, ada_w, ada_b.reshape(depth, 1, n))


def _in_body(x_ref, nw_ref, sh_ref, sc_ref, w_ref, wdt_ref, o_ref, dt_ref, h_ref):
    @pl.when(pl.program_id(1) == 0)
    def _():
        h = _rms(x_ref[...]) * nw_ref[...]
        h = (h * (1.0 + sc_ref[0]) + sh_ref[0]).astype(BF16)
        h_ref[...] = h
        dt_ref[...] = jnp.dot(h, wdt_ref[...], preferred_element_type=F32)

    o_ref[...] = jnp.dot(h_ref[...], w_ref[...], preferred_element_type=F32).astype(BF16)


def _in_call(x2, nw, ada3, w_main, w_dt, *, seq, tm, tn):
    t, d = x2.shape
    n = w_main.shape[1]
    per_b = seq // tm
    return pl.pallas_call(
        _in_body,
        grid=(t // tm, n // tn),
        in_specs=[pl.BlockSpec((tm, d), lambda i, j: (i, 0)),
                  pl.BlockSpec((1, d), lambda i, j: (0, 0)),
                  pl.BlockSpec((1, 1, d), lambda i, j: (i // per_b, 0, 0)),
                  pl.BlockSpec((1, 1, d), lambda i, j: (i // per_b, 0, 1)),
                  pl.BlockSpec((d, tn), lambda i, j: (0, j)),
                  pl.BlockSpec((d, LANES), lambda i, j: (0, 0))],
        out_specs=[pl.BlockSpec((tm, tn), lambda i, j: (i, j)),
                   pl.BlockSpec((tm, LANES), lambda i, j: (i, 0))],
        out_shape=[jax.ShapeDtypeStruct((t, n), BF16),
                   jax.ShapeDtypeStruct((t, LANES), F32)],
        scratch_shapes=[pltpu.VMEM((tm, d), BF16)],
        compiler_params=_cparams(2),
        name="in_proj",
    )(x2, nw, ada3, ada3, w_main, w_dt)


def _ret_tables(seq):
    half = RET_QK // 2
    inv_freq = ROPE_BASE ** (-np.arange(half, dtype=np.float64) / half)
    ang = np.arange(seq, dtype=np.float64)[:, None] * inv_freq[None, :]
    cos, sin = np.cos(ang), np.sin(ang)
    cos2 = np.concatenate([cos, cos], axis=1)
    sin2 = np.concatenate([-sin, sin], axis=1)
    lg = np.log1p(-np.exp(np.linspace(math.log(1.0 / 32), math.log(1.0 / 512), RET_HEADS)))
    idx = np.arange(CHUNK, dtype=np.float64)
    diff = idx[:, None] - idx[None, :]
    intra = np.where(diff[None] >= 0, np.exp(diff[None] * lg[:, None, None]), 0.0)
    cross_q = np.exp((idx[None, :, None] + 1.0) * lg[:, None, None])
    state_k = np.exp((CHUNK - 1.0 - idx[None, :, None]) * lg[:, None, None])
    chunk_decay = np.exp(CHUNK * lg)[:, None, None]
    as32 = lambda a: jnp.asarray(a, F32)
    return (as32(cos2), as32(sin2), as32(intra),
            as32(np.broadcast_to(cross_q, (RET_HEADS, CHUNK, RET_V))),
            as32(np.broadcast_to(state_k, (RET_HEADS, CHUNK, RET_QK))),
            as32(np.broadcast_to(chunk_decay, (RET_HEADS, 1, RET_V))))


def _ret_body(q_ref, k_ref, v_ref, g_ref, cos_ref, sin_ref, intra_ref, cq_ref, sk_ref, cd_ref,
              o_ref, r_ref):
    n_chunks = q_ref.shape[0] // CHUNK
    r_ref[...] = jnp.zeros_like(r_ref)

    def step(c, carry):
        rows = pl.ds(pl.multiple_of(c * CHUNK, CHUNK), CHUNK)
        cos = cos_ref[rows, :]
        sin = sin_ref[rows, :]
        for h in range(RET_HEADS):
            qk_cols = slice(h * RET_QK, (h + 1) * RET_QK)
            v_cols = slice(h * RET_V, (h + 1) * RET_V)
            q = q_ref[rows, qk_cols].astype(F32)
            k = k_ref[rows, qk_cols].astype(F32)
            qr = q * cos + pltpu.roll(q, RET_QK // 2, 1) * sin
            kr = (k * cos + pltpu.roll(k, RET_QK // 2, 1) * sin) * (RET_QK ** -0.5)
            qb = qr.astype(BF16)
            v = v_ref[rows, v_cols]
            scores = lax.dot_general(qb, kr.astype(BF16), (((1,), (1,)), ((), ())),
                                     preferred_element_type=F32) * intra_ref[h]
            inner = jnp.dot(scores.astype(BF16), v, preferred_element_type=F32)
            state = r_ref[h]
            cross = jnp.dot(qb, state.astype(BF16), preferred_element_type=F32) * cq_ref[h]
            ks_t = (kr * sk_ref[h]).T.astype(BF16)
            r_ref[h] = state * cd_ref[h] + jnp.dot(ks_t, v, preferred_element_type=F32)
            y = inner + cross
            dlt = y - jnp.mean(y, axis=-1, keepdims=True)
            yn = dlt * lax.rsqrt(jnp.mean(dlt * dlt, axis=-1, keepdims=True) + EPS)
            o_ref[rows, v_cols] = (_silu(g_ref[rows, v_cols].astype(F32)) * yn).astype(BF16)
        return carry

    lax.fori_loop(0, n_chunks, step, 0, unroll=2)


def _ret_call(proj, tables, *, batch, seq):
    cos2, sin2, intra, cq, sk, cd = tables
    t = proj.shape[0]
    wqk, wv = RET_HEADS * RET_QK, RET_HEADS * RET_V
    whole = lambda a: pl.BlockSpec(a.shape, lambda b: (0,) * a.ndim)
    return pl.pallas_call(
        _ret_body,
        grid=(batch,),
        in_specs=[pl.BlockSpec((seq, wqk), lambda b: (b, P_RQ // wqk)),
                  pl.BlockSpec((seq, wqk), lambda b: (b, P_RK // wqk)),
                  pl.BlockSpec((seq, wv), lambda b: (b, P_RV // wv)),
                  pl.BlockSpec((seq, wv), lambda b: (b, P_RG // wv)),
                  whole(cos2), whole(sin2), whole(intra), whole(cq), whole(sk), whole(cd)],
        out_specs=pl.BlockSpec((seq, wv), lambda b: (b, 0)),
        out_shape=jax.ShapeDtypeStruct((t, wv), BF16),
        scratch_shapes=[pltpu.VMEM((RET_HEADS, RET_QK, RET_V), F32)],
        compiler_params=_cparams(1),
        name="retention",
    )(proj, proj, proj, proj, cos2, sin2, intra, cq, sk, cd)


def _pair(col_src, h0, shape):
    lane = lax.broadcasted_iota(jnp.int32, shape, 1)
    a = jnp.broadcast_to(col_src[:, h0:h0 + 1], shape)
    b = jnp.broadcast_to(col_src[:, h0 + 1:h0 + 2], shape)
    return jnp.where(lane < SSM_P, a, b)


def _ssd_body(z_ref, xs_ref, bm_ref, cm_ref, dt_ref, cwx_ref, cwb_ref, cwc_ref, cbx_ref, cbb_ref,
              cbc_ref, dtb_ref, alog_ref, dsk_ref, nw_ref, o_ref, st_ref):
    n_chunks = z_ref.shape[0] // CHUNK
    st_ref[...] = jnp.zeros_like(st_ref)
    row = lax.broadcasted_iota(jnp.int32, (CHUNK, CHUNK), 0)
    col = lax.broadcasted_iota(jnp.int32, (CHUNK, CHUNK), 1)
    causal = row >= col
    lower = causal.astype(F32)
    upper = (row <= col).astype(F32)
    lane = lax.broadcasted_iota(jnp.int32, (CHUNK, LANES), 1)
    first = lane < SSM_P
    hi = lax.Precision.HIGHEST

    wrow = lax.broadcasted_iota(jnp.int32, (CHUNK, 2 * CHUNK), 0)
    wcol = lax.broadcasted_iota(jnp.int32, (CHUNK, 2 * CHUNK), 1)

    def conv(x_ref, w_ref, b_ref, rows, win, shifts):
        window = x_ref[win, :]
        acc = b_ref[...] + x_ref[rows, :].astype(F32) * w_ref[SSM_CONV - 1:SSM_CONV, :]
        for k in range(SSM_CONV - 1):
            acc = acc + jnp.dot(shifts[k], window, preferred_element_type=F32) * w_ref[k:k + 1, :]
        return _silu(acc)

    def step(c, carry):
        rows = pl.ds(pl.multiple_of(c * CHUNK, CHUNK), CHUNK)
        first_chunk = c == 0
        win = pl.ds(pl.multiple_of(jnp.maximum(c - 1, 0) * CHUNK, CHUNK), 2 * CHUNK)
        cur_off = jnp.where(first_chunk, 0, CHUNK)
        shifts = [jnp.where(wcol == wrow + (cur_off - (SSM_CONV - 1 - k)), 1.0, 0.0).astype(BF16)
                  for k in range(SSM_CONV - 1)]
        xs = conv(xs_ref, cwx_ref, cbx_ref, rows, win, shifts)
        bm = conv(bm_ref, cwb_ref, cbb_ref, rows, win, shifts)
        cm = conv(cm_ref, cwc_ref, cbc_ref, rows, win, shifts)
        dt_raw = dt_ref[rows, :] + dtb_ref[...]
        dt = jnp.maximum(dt_raw, 0.0) + jnp.log(1.0 + jnp.exp(-jnp.abs(dt_raw)))
        da = dt * (-jnp.exp(alog_ref[...]))
        cs_col = jnp.dot(lower, da, precision=hi, preferred_element_type=F32)
        cs_row = jnp.dot(da.T, upper, precision=hi, preferred_element_type=F32)
        cs_last = cs_col[CHUNK - 1:CHUNK, :]
        to_end = jnp.exp(cs_last - cs_col)
        from_start = jnp.exp(cs_col)
        end_decay = jnp.exp(cs_last)
        dsk = dsk_ref[...]
        for g in range(SSM_G):
            bg = bm[:, g * SSM_N:(g + 1) * SSM_N]
            cg = cm[:, g * SSM_N:(g + 1) * SSM_N].astype(BF16)
            cb = lax.dot_general(cg, bg.astype(BF16), (((1,), (1,)), ((), ())),
                                 preferred_element_type=F32)
            bg_t = bg.T.astype(BF16)
            ys, sq = [], []
            for pr in range(2):
                h0 = g * 4 + pr * 2
                blk = slice(h0 * SSM_P, (h0 + 2) * SSM_P)
                xh = xs[:, blk]
                xdt = xh * _pair(dt, h0, (CHUNK, LANES))
                xdt_b = xdt.astype(BF16)
                yd = []
                for hh in (h0, h0 + 1):
                    seg = cs_col[:, hh:hh + 1] - cs_row[hh:hh + 1, :]
                    lmat = jnp.exp(jnp.where(causal, seg, -jnp.inf))
                    yd.append(jnp.dot((cb * lmat).astype(BF16), xdt_b, preferred_element_type=F32))
                y_diag = jnp.where(first, yd[0], yd[1])
                state = st_ref[g, pr]
                y_off = jnp.dot(cg, state.astype(BF16), preferred_element_type=F32)
                y_off = y_off * _pair(from_start, h0, (CHUNK, LANES))
                upd = (xdt * _pair(to_end, h0, (CHUNK, LANES))).astype(BF16)
                st_ref[g, pr] = (state * _pair(end_decay, h0, (1, LANES))
                                 + jnp.dot(bg_t, upd, preferred_element_type=F32))
                y = y_diag + y_off + xh * _pair(dsk, h0, (1, LANES))
                y = y * _silu(z_ref[rows, blk].astype(F32))
                ys.append(y)
                sq.append(y * y)
            ssq = _row_sums(jnp.concatenate(sq, axis=1))
            inv = lax.rsqrt(ssq * (1.0 / (4 * SSM_P)) + EPS)
            for pr in range(2):
                h0 = g * 4 + pr * 2
                blk = slice(h0 * SSM_P, (h0 + 2) * SSM_P)
                o_ref[rows, blk] = (ys[pr] * inv * nw_ref[:, blk]).astype(BF16)
        return carry

    lax.fori_loop(0, n_chunks, step, 0)


def _ssd_call(proj, dt, conv_w, conv_b, dt_bias, a_log, d_skip, norm_w, *, batch, seq):
    t = proj.shape[0]
    gn = SSM_G * SSM_N
    pad = lambda v: jnp.pad(v, (0, LANES - SSM_HEADS)).reshape(1, LANES)
    conv_b2 = conv_b.reshape(1, -1)
    return pl.pallas_call(
        _ssd_body,
        grid=(batch,),
        in_specs=[pl.BlockSpec((seq, SSM_INNER), lambda b: (b, P_SZ // SSM_INNER)),
                  pl.BlockSpec((seq, SSM_INNER), lambda b: (b, P_SX // SSM_INNER)),
                  pl.BlockSpec((seq, gn), lambda b: (b, P_SB // gn)),
                  pl.BlockSpec((seq, gn), lambda b: (b, P_SC // gn)),
                  pl.BlockSpec((seq, LANES), lambda b: (b, 0)),
                  pl.BlockSpec((SSM_CONV, SSM_INNER), lambda b: (0, 0)),
                  pl.BlockSpec((SSM_CONV, gn), lambda b: (0, SSM_INNER // gn)),
                  pl.BlockSpec((SSM_CONV, gn), lambda b: (0, SSM_INNER // gn + 1)),
                  pl.BlockSpec((1, SSM_INNER), lambda b: (0, 0)),
                  pl.BlockSpec((1, gn), lambda b: (0, SSM_INNER // gn)),
                  pl.BlockSpec((1, gn), lambda b: (0, SSM_INNER // gn + 1)),
                  pl.BlockSpec((1, LANES), lambda b: (0, 0)),
                  pl.BlockSpec((1, LANES), lambda b: (0, 0)),
                  pl.BlockSpec((1, LANES), lambda b: (0, 0)),
                  pl.BlockSpec((1, SSM_INNER), lambda b: (0, 0))],
        out_specs=pl.BlockSpec((seq, SSM_INNER), lambda b: (b, 0)),
        out_shape=jax.ShapeDtypeStruct((t, SSM_INNER), BF16),
        scratch_shapes=[pltpu.VMEM((SSM_G, 2, SSM_N, LANES), F32)],
        compiler_params=_cparams(1),
        name="ssd",
    )(proj, proj, proj, proj, dt, conv_w, conv_w, conv_w, conv_b2, conv_b2, conv_b2,
      pad(dt_bias), pad(a_log), pad(d_skip), norm_w.reshape(1, -1))


def _att_proj_body(x_ref, nw_ref, sh_ref, sc_ref, w_ref, o_ref):
    h = _rms(x_ref[...]) * nw_ref[...]
    h = (h * (1.0 + sc_ref[0]) + sh_ref[0]).astype(BF16)
    o_ref[...] = jnp.dot(h, w_ref[...], preferred_element_type=F32)


def _att_proj_call(x2, nw, ada3, w_att, *, seq, tm):
    t, d = x2.shape
    n = w_att.shape[1]
    per_b = seq // tm
    return pl.pallas_call(
        _att_proj_body,
        grid=(t // tm,),
        in_specs=[pl.BlockSpec((tm, d), lambda i: (i, 0)),
                  pl.BlockSpec((1, d), lambda i: (0, 0)),
                  pl.BlockSpec((1, 1, d), lambda i: (i // per_b, 0, 0)),
                  pl.BlockSpec((1, 1, d), lambda i: (i // per_b, 0, 1)),
                  pl.BlockSpec((d, n), lambda i: (0, 0))],
        out_specs=pl.BlockSpec((tm, n), lambda i: (i, 0)),
        out_shape=jax.ShapeDtypeStruct((t, n), F32),
        compiler_params=_cparams(1),
        name="att_proj",
    )(x2, nw, ada3, ada3, w_att)


def _band_body(q0_ref, q1_ref, k0_ref, k1_ref, v0_ref, v1_ref, o_ref, lse_ref, *, dil):
    seq = q0_ref.shape[0]
    n_q = seq // dil // CHUNK
    qkv = ((q0_ref, k0_ref, v0_ref), (q1_ref, k1_ref, v1_ref))
    first = lax.broadcasted_iota(jnp.int32, (CHUNK, LANES), 1) < ATT_E

    def band_mask(nk):
        pos = lax.broadcasted_iota(jnp.int32, (2 * CHUNK, nk), 0) & (CHUNK - 1)
        dist = pos - lax.broadcasted_iota(jnp.int32, (2 * CHUNK, nk), 1) + (nk - CHUNK)
        return (dist >= 0) & (dist <= CHUNK)

    ok_cur, ok_both = band_mask(CHUNK), band_mask(2 * CHUNK)

    def rows(r, p0, n):
        return pl.ds(r + dil * p0, n, stride=dil) if dil > 1 else pl.ds(r + p0, n)

    for r in range(dil):
        for j in range(n_q):
            q_rows = rows(r, j * CHUNK, CHUNK)
            k_rows = rows(r, (j - 1) * CHUNK, 2 * CHUNK) if j else q_rows
            ok = ok_both if j else ok_cur
            for pr, (q_ref, k_ref, v_ref) in enumerate(qkv):
                q = q_ref[q_rows, :] * (ATT_E ** -0.5)
                qs = jnp.concatenate([jnp.where(first, q, 0.0), jnp.where(first, 0.0, q)],
                                     axis=0).astype(BF16)
                k = k_ref[k_rows, :].astype(BF16)
                v = v_ref[k_rows, :]
                s = lax.dot_general(qs, k, (((1,), (1,)), ((), ())), preferred_element_type=F32)
                s = jnp.where(ok, s, -jnp.inf)
                m = jnp.max(s, axis=-1, keepdims=True)
                p = jnp.exp(s - m)
                if j:
                    v = jnp.concatenate([v, jnp.ones_like(v)], axis=1)
                    pv = jnp.dot(p.astype(BF16), v.astype(BF16), preferred_element_type=F32)
                    l = pv[:, LANES:]
                    out = pv[:, :LANES] / l
                else:
                    l = jnp.sum(p, axis=-1, keepdims=True)
                    out = jnp.dot(p.astype(BF16), v.astype(BF16), preferred_element_type=F32) / l
                lse = m + jnp.log(l)
                o_ref[pr, q_rows, :] = jnp.where(first, out[:CHUNK], out[CHUNK:])
                lse_ref[pr, q_rows, :] = jnp.where(first, lse[:CHUNK], lse[CHUNK:])


def _band_call(att, gi, *, batch, seq):
    window, dil = ATT_PATTERNS[gi]
    assert window // dil == CHUNK and seq % (dil * CHUNK) == 0
    t = att.shape[0]
    pairs = ATT_SLOTS // 2
    col0 = gi * 3 * pairs
    in_specs = [pl.BlockSpec((seq, LANES), lambda b, c=col0 + part * pairs + pr: (b, c))
                for pr in range(pairs) for part in range(3)]
    order = [0, 3, 1, 4, 2, 5]
    out_spec = pl.BlockSpec((pairs, seq, LANES), lambda b: (0, b, 0))
    out_shape = jax.ShapeDtypeStruct((pairs, t, LANES), F32)
    return pl.pallas_call(
        functools.partial(_band_body, dil=dil),
        grid=(batch,),
        in_specs=[in_specs[i] for i in order],
        out_specs=[out_spec, out_spec],
        out_shape=[out_shape, out_shape],
        compiler_params=_cparams(1),
        name=f"band_attention_{gi}",
    )(*([att] * 6))


def _merge_body(x_ref, g0_ref, g1_ref, g2_ref, yr_ref, ys_ref, a0_ref, a1_ref, a2_ref,
                l0_ref, l1_ref, l2_ref, gate_ref, wr_ref, ws_ref, wa_ref, wo_ref, o_ref):
    att_proj = None
    for pr in range(ATT_SLOTS // 2):
        lses = [r[pr] for r in (l0_ref, l1_ref, l2_ref)]
        top = jnp.maximum(jnp.maximum(lses[0], lses[1]), lses[2])
        wts = [jnp.exp(l - top) for l in lses]
        num = sum(w * a[pr] for w, a in zip(wts, (a0_ref, a1_ref, a2_ref)))
        y_att = (num / (wts[0] + wts[1] + wts[2])).astype(BF16)
        part = jnp.dot(y_att, wa_ref[pr * LANES:(pr + 1) * LANES, :], preferred_element_type=F32)
        att_proj = part if att_proj is None else att_proj + part

    def branch(g_ref, y, w_ref):
        return jax.nn.sigmoid(g_ref[...].astype(F32)) * jnp.dot(
            y, w_ref[...], preferred_element_type=F32)

    mixed = (branch(g0_ref, yr_ref[...], wr_ref) + branch(g1_ref, ys_ref[...], ws_ref)
             + jax.nn.sigmoid(g2_ref[...].astype(F32)) * att_proj)
    upd = jnp.dot(mixed.astype(BF16), wo_ref[...], preferred_element_type=F32)
    o_ref[...] = x_ref[...] + gate_ref[0] * upd


def _merge_call(x2, proj, y_ret, y_ssm, att, ada3, wr, ws, wa, wo, *, seq, tm):
    t, d = x2.shape
    per_b = seq // tm
    const = lambda i: (0, 0)
    rows = lambda a: pl.BlockSpec((tm, a.shape[1]), lambda i: (i, 0))
    pair_rows = lambda a: pl.BlockSpec((a.shape[0], tm, a.shape[2]), lambda i: (0, i, 0))
    att_o = [o for o, _ in att]
    att_l = [l for _, l in att]
    return pl.pallas_call(
        _merge_body,
        grid=(t // tm,),
        in_specs=[pl.BlockSpec((tm, d), lambda i: (i, 0)),
                  pl.BlockSpec((tm, d), lambda i: (i, P_GATE // d)),
                  pl.BlockSpec((tm, d), lambda i: (i, P_GATE // d + 1)),
                  pl.BlockSpec((tm, d), lambda i: (i, P_GATE // d + 2)),
                  rows(y_ret), rows(y_ssm)]
                 + [pair_rows(a) for a in att_o] + [pair_rows(a) for a in att_l]
                 + [pl.BlockSpec((1, 1, d), lambda i: (i // per_b, 0, 2)),
                    pl.BlockSpec(wr.shape, const),
                    pl.BlockSpec(ws.shape, const),
                    pl.BlockSpec(wa.shape, const),
                    pl.BlockSpec(wo.shape, const)],
        out_specs=pl.BlockSpec((tm, d), lambda i: (i, 0)),
        out_shape=jax.ShapeDtypeStruct((t, d), F32),
        compiler_params=_cparams(1),
        name="merge_out",
    )(x2, proj, proj, proj, y_ret, y_ssm, *att_o, *att_l, ada3, wr, ws, wa, wo)


def _ffn_body(x_ref, nw_ref, sh_ref, sc_ref, gate_ref, w1_ref, w3_ref, w2_ref, fw_ref, o_ref,
              *, final_norm):
    x = x_ref[...]
    h = _rms(x) * nw_ref[...]
    h = (h * (1.0 + sc_ref[0]) + sh_ref[0]).astype(BF16)
    a = jnp.dot(h, w1_ref[...], preferred_element_type=F32)
    b = jnp.dot(h, w3_ref[...], preferred_element_type=F32)
    act = (_silu(a) * b).astype(BF16)
    y = x + gate_ref[0] * jnp.dot(act, w2_ref[...], preferred_element_type=F32)
    if final_norm:
        y = _rms(y) * fw_ref[...]
    o_ref[...] = y


def _ffn_call(x2, nw, ada3, w1, w3, w2, final_w, *, seq, tm, final_norm):
    t, d = x2.shape
    per_b = seq // tm
    const = lambda i: (0, 0)
    resident = dict(pipeline_mode=pl.Buffered(1))
    return pl.pallas_call(
        functools.partial(_ffn_body, final_norm=final_norm),
        grid=(t // tm,),
        in_specs=[pl.BlockSpec((tm, d), lambda i: (i, 0)),
                  pl.BlockSpec((1, d), const),
                  pl.BlockSpec((1, 1, d), lambda i: (i // per_b, 0, 3)),
                  pl.BlockSpec((1, 1, d), lambda i: (i // per_b, 0, 4)),
                  pl.BlockSpec((1, 1, d), lambda i: (i // per_b, 0, 5)),
                  pl.BlockSpec(w1.shape, const, **resident),
                  pl.BlockSpec(w3.shape, const, **resident),
                  pl.BlockSpec(w2.shape, const, **resident),
                  pl.BlockSpec((1, d), const)],
        out_specs=pl.BlockSpec((tm, d), lambda i: (i, 0)),
        out_shape=jax.ShapeDtypeStruct((t, d), F32),
        compiler_params=_cparams(1),
        name="ffn",
    )(x2, nw, ada3, ada3, ada3, w1, w3, w2, final_w)


def _tile(seq, want):
    return min(seq, want)


def kernel(x, c, ada_w, ada_b, norm1_w, w_in, ssm_conv_w, ssm_conv_b, ssm_dt_bias, ssm_a_log,
           ssm_d, ssm_norm_w, w_br_ret, w_br_ssm, w_br_att, w_out, norm2_w, w_ff1, w_ff3, w_ff2,
           final_norm_w):
    batch, seq, d = x.shape
    depth = ada_w.shape[0]
    assert d == D_MODEL and seq % (2 * CHUNK) == 0 and w_in.shape[2] == _O_END
    t = batch * seq
    ada = _ada_call(c, ada_w.astype(BF16), ada_b)
    tables = _ret_tables(seq)
    x2 = x.reshape(t, d)
    final_w = final_norm_w.reshape(1, d)
    for i in range(depth):
        ada3 = ada[i].reshape(batch, 1, 6 * d)
        w = w_in[i]
        w_main = jnp.concatenate([w[:, _O_GATE:_O_END], w[:, :_O_DT]], axis=1).astype(BF16)
        w_dt = jnp.pad(w[:, _O_DT:_O_AQ], ((0, 0), (0, LANES - SSM_HEADS))).astype(BF16)
        gw = ATT_SLOTS * ATT_E
        w_att = jnp.concatenate(
            [w[:, _O_AQ + part * ATT_W + g * gw:_O_AQ + part * ATT_W + (g + 1) * gw]
             for g in range(len(ATT_PATTERNS)) for part in range(3)], axis=1).astype(BF16)
        nw1 = norm1_w[i].reshape(1, d)
        proj, dt = _in_call(x2, nw1, ada3, w_main, w_dt, seq=seq, tm=_tile(seq, 1024), tn=3072)
        qkv = _att_proj_call(x2, nw1, ada3, w_att, seq=seq, tm=_tile(seq, 512))
        y_ret = _ret_call(proj, tables, batch=batch, seq=seq)
        y_ssm = _ssd_call(proj, dt, ssm_conv_w[i], ssm_conv_b[i], ssm_dt_bias[i], ssm_a_log[i],
                          ssm_d[i], ssm_norm_w[i], batch=batch, seq=seq)
        att = [_band_call(qkv, gi, batch=batch, seq=seq) for gi in range(len(ATT_PATTERNS))]
        x2 = _merge_call(x2, proj, y_ret, y_ssm, att, ada3, w_br_ret[i].astype(BF16),
                         w_br_ssm[i].astype(BF16), w_br_att[i].astype(BF16),
                         w_out[i].astype(BF16), seq=seq, tm=_tile(seq, 512))
        x2 = _ffn_call(x2, norm2_w[i].reshape(1, d), ada3, w_ff1[i].astype(BF16),
                       w_ff3[i].astype(BF16), w_ff2[i].astype(BF16), final_w,
                       seq=seq, tm=_tile(seq, 512), final_norm=(i == depth - 1))
    return x2.reshape(batch, seq, d)
```

```python
import functools
import math

import numpy as np
import jax
import jax.numpy as jnp
from jax import lax
from jax.experimental import pallas as pl
from jax.experimental.pallas import tpu as pltpu

F32 = jnp.float32
BF16 = jnp.bfloat16

D_MODEL = 1024
EPS = 1e-6
CHUNK = 128
RET_HEADS = 4
RET_QK = 128
RET_V = 256
ROPE_BASE = 10000.0
SSM_HEADS = 16
SSM_P = 64
SSM_G = 4
SSM_N = 128
SSM_CONV = 4
SSM_INNER = SSM_HEADS * SSM_P
ATT_PATTERNS = ((128, 1), (512, 4), (2048, 16))
ATT_SLOTS = 4
ATT_E = 64
ATT_W = len(ATT_PATTERNS) * ATT_SLOTS * ATT_E
N_BRANCHES = 3

_O_RQ = 0
_O_SXBC = 4096
_O_DT = 6144
_O_AQ = 6160
_O_GATE = 8464
_O_END = 11536
P_GATE = 0
P_RQ = 3072
P_RK = 3584
P_RV = 4096
P_RG = 5120
P_SZ = 6144
P_SX = 7168
P_SB = 8192
P_SC = 8704

LANES = 128
VMEM_LIMIT = 56 * 1024 * 1024


def _cparams(n_axes):
    return pltpu.CompilerParams(
        dimension_semantics=("arbitrary",) * n_axes, vmem_limit_bytes=VMEM_LIMIT)


def _silu(v):
    return v * jax.nn.sigmoid(v)


def _row_sums(v):
    hi = v.astype(BF16)
    lo = (v - hi.astype(F32)).astype(BF16)
    both = jnp.concatenate([hi, lo], axis=1)
    return jnp.dot(both, jnp.ones((both.shape[1], LANES), BF16), preferred_element_type=F32)


def _rms(v):
    return v * lax.rsqrt(jnp.mean(v * v, axis=-1, keepdims=True) + EPS)


def _ada_body(c_ref, w_ref, b_ref, o_ref):
    s = _silu(c_ref[...])
    o_ref[0] = jnp.dot(s.astype(BF16), w_ref[0], preferred_element_type=F32) + b_ref[0]


def _ada_call(c, ada_w, ada_b):
    depth, d, n = ada_w.shape
    b = c.shape[0]
    return pl.pallas_call(
        _ada_body,
        grid=(depth, n // d),
        in_specs=[pl.BlockSpec((b, d), lambda l, j: (0, 0)),
                  pl.BlockSpec((1, d, d), lambda l, j: (l, 0, j)),
                  pl.BlockSpec((1, 1, d), lambda l, j: (l, 0, j))],
        out_specs=pl.BlockSpec((1, b, d), lambda l, j: (l, 0, j)),
        out_shape=jax.ShapeDtypeStruct((depth, b, n), F32),
        compiler_params=_cparams(2),
        name="ada",
    )(c, ada_w, ada_b.reshape(depth, 1, n))


def _in_body(x_ref, nw_ref, sh_ref, sc_ref, w_ref, wdt_ref, o_ref, dt_ref, h_ref):
    @pl.when(pl.program_id(1) == 0)
    def _():
        h = _rms(x_ref[...]) * nw_ref[...]
        h = (h * (1.0 + sc_ref[0]) + sh_ref[0]).astype(BF16)
        h_ref[...] = h
        dt_ref[...] = jnp.dot(h, wdt_ref[...], preferred_element_type=F32)

    o_ref[...] = jnp.dot(h_ref[...], w_ref[...], preferred_element_type=F32).astype(BF16)


def _in_call(x2, nw, ada3, w_main, w_dt, *, seq, tm, tn):
    t, d = x2.shape
    n = w_main.shape[1]
    per_b = seq // tm
    return pl.pallas_call(
        _in_body,
        grid=(t // tm, n // tn),
        in_specs=[pl.BlockSpec((tm, d), lambda i, j: (i, 0)),
                  pl.BlockSpec((1, d), lambda i, j: (0, 0)),
                  pl.BlockSpec((1, 1, d), lambda i, j: (i // per_b, 0, 0)),
                  pl.BlockSpec((1, 1, d), lambda i, j: (i // per_b, 0, 1)),
                  pl.BlockSpec((d, tn), lambda i, j: (0, j)),
                  pl.BlockSpec((d, LANES), lambda i, j: (0, 0))],
        out_specs=[pl.BlockSpec((tm, tn), lambda i, j: (i, j)),
                   pl.BlockSpec((tm, LANES), lambda i, j: (i, 0))],
        out_shape=[jax.ShapeDtypeStruct((t, n), BF16),
                   jax.ShapeDtypeStruct((t, LANES), F32)],
        scratch_shapes=[pltpu.VMEM((tm, d), BF16)],
        compiler_params=_cparams(2),
        name="in_proj",
    )(x2, nw, ada3, ada3, w_main, w_dt)


def _ret_tables(seq):
    half = RET_QK // 2
    inv_freq = ROPE_BASE ** (-np.arange(half, dtype=np.float64) / half)
    ang = np.arange(seq, dtype=np.float64)[:, None] * inv_freq[None, :]
    cos, sin = np.cos(ang), np.sin(ang)
    cos2 = np.concatenate([cos, cos], axis=1)
    sin2 = np.concatenate([-sin, sin], axis=1)
    lg = np.log1p(-np.exp(np.linspace(math.log(1.0 / 32), math.log(1.0 / 512), RET_HEADS)))
    idx = np.arange(CHUNK, dtype=np.float64)
    diff = idx[:, None] - idx[None, :]
    intra = np.where(diff[None] >= 0, np.exp(diff[None] * lg[:, None, None]), 0.0)
    cross_q = np.exp((idx[None, :, None] + 1.0) * lg[:, None, None])
    state_k = np.exp((CHUNK - 1.0 - idx[None, :, None]) * lg[:, None, None])
    chunk_decay = np.exp(CHUNK * lg)[:, None, None]
    as32 = lambda a: jnp.asarray(a, F32)
    return (as32(cos2), as32(sin2), as32(intra),
            as32(np.broadcast_to(cross_q, (RET_HEADS, CHUNK, RET_V))),
            as32(np.broadcast_to(state_k, (RET_HEADS, CHUNK, RET_QK))),
            as32(np.broadcast_to(chunk_decay, (RET_HEADS, 1, RET_V))))


def _ret_body(q_ref, k_ref, v_ref, g_ref, cos_ref, sin_ref, intra_ref, cq_ref, sk_ref, cd_ref,
              o_ref, r_ref):
    n_chunks = q_ref.shape[0] // CHUNK
    r_ref[...] = jnp.zeros_like(r_ref)

    def step(c, carry):
        rows = pl.ds(pl.multiple_of(c * CHUNK, CHUNK), CHUNK)
        cos = cos_ref[rows, :]
        sin = sin_ref[rows, :]
        for h in range(RET_HEADS):
            qk_cols = slice(h * RET_QK, (h + 1) * RET_QK)
            v_cols = slice(h * RET_V, (h + 1) * RET_V)
            q = q_ref[rows, qk_cols].astype(F32)
            k = k_ref[rows, qk_cols].astype(F32)
            qr = q * cos + pltpu.roll(q, RET_QK // 2, 1) * sin
            kr = (k * cos + pltpu.roll(k, RET_QK // 2, 1) * sin) * (RET_QK ** -0.5)
            qb = qr.astype(BF16)
            v = v_ref[rows, v_cols]
            scores = lax.dot_general(qb, kr.astype(BF16), (((1,), (1,)), ((), ())),
                                     preferred_element_type=F32) * intra_ref[h]
            inner = jnp.dot(scores.astype(BF16), v, preferred_element_type=F32)
            state = r_ref[h]
            cross = jnp.dot(qb, state.astype(BF16), preferred_element_type=F32) * cq_ref[h]
            ks_t = (kr * sk_ref[h]).T.astype(BF16)
            r_ref[h] = state * cd_ref[h] + jnp.dot(ks_t, v, preferred_element_type=F32)
            y = inner + cross
            dlt = y - jnp.mean(y, axis=-1, keepdims=True)
            yn = dlt * lax.rsqrt(jnp.mean(dlt * dlt, axis=-1, keepdims=True) + EPS)
            o_ref[rows, v_cols] = (_silu(g_ref[rows, v_cols].astype(F32)) * yn).astype(BF16)
        return carry

    lax.fori_loop(0, n_chunks, step, 0, unroll=2)


def _ret_call(proj, tables, *, batch, seq):
    cos2, sin2, intra, cq, sk, cd = tables
    t = proj.shape[0]
    wqk, wv = RET_HEADS * RET_QK, RET_HEADS * RET_V
    whole = lambda a: pl.BlockSpec(a.shape, lambda b: (0,) * a.ndim)
    return pl.pallas_call(
        _ret_body,
        grid=(batch,),
        in_specs=[pl.BlockSpec((seq, wqk), lambda b: (b, P_RQ // wqk)),
                  pl.BlockSpec((seq, wqk), lambda b: (b, P_RK // wqk)),
                  pl.BlockSpec((seq, wv), lambda b: (b, P_RV // wv)),
                  pl.BlockSpec((seq, wv), lambda b: (b, P_RG // wv)),
                  whole(cos2), whole(sin2), whole(intra), whole(cq), whole(sk), whole(cd)],
        out_specs=pl.BlockSpec((seq, wv), lambda b: (b, 0)),
        out_shape=jax.ShapeDtypeStruct((t, wv), BF16),
        scratch_shapes=[pltpu.VMEM((RET_HEADS, RET_QK, RET_V), F32)],
        compiler_params=_cparams(1),
        name="retention",
    )(proj, proj, proj, proj, cos2, sin2, intra, cq, sk, cd)


def _pair(col_src, h0, shape):
    lane = lax.broadcasted_iota(jnp.int32, shape, 1)
    if shape[0] == 1:
        a = jnp.broadcast_to(col_src[:, h0:h0 + 1], shape)
        b = jnp.broadcast_to(col_src[:, h0 + 1:h0 + 2], shape)
        return jnp.where(lane < SSM_P, a, b)
    return jnp.take_along_axis(col_src, h0 + (lane >> 6), axis=1)


def _ssd_body(z_ref, xs_ref, bm_ref, cm_ref, dt_ref, cwx_ref, cwb_ref, cwc_ref, cbx_ref, cbb_ref,
              cbc_ref, dtb_ref, alog_ref, dsk_ref, nw_ref, o_ref, st_ref):
    n_chunks = z_ref.shape[0] // CHUNK
    st_ref[...] = jnp.zeros_like(st_ref)
    row = lax.broadcasted_iota(jnp.int32, (CHUNK, CHUNK), 0)
    col = lax.broadcasted_iota(jnp.int32, (CHUNK, CHUNK), 1)
    causal = row >= col
    lower = causal.astype(F32)
    upper = (row <= col).astype(F32)
    lane = lax.broadcasted_iota(jnp.int32, (CHUNK, LANES), 1)
    first = lane < SSM_P
    hi = lax.Precision.HIGHEST

    wrow = lax.broadcasted_iota(jnp.int32, (CHUNK, 2 * CHUNK), 0)
    wcol = lax.broadcasted_iota(jnp.int32, (CHUNK, 2 * CHUNK), 1)

    def conv(x_ref, w_ref, b_ref, rows, win, shifts):
        window = x_ref[win, :]
        acc = b_ref[...] + x_ref[rows, :].astype(F32) * w_ref[SSM_CONV - 1:SSM_CONV, :]
        for k in range(SSM_CONV - 1):
            acc = acc + jnp.dot(shifts[k], window, preferred_element_type=F32) * w_ref[k:k + 1, :]
        return _silu(acc)

    def step(c, carry):
        rows = pl.ds(pl.multiple_of(c * CHUNK, CHUNK), CHUNK)
        first_chunk = c == 0
        win = pl.ds(pl.multiple_of(jnp.maximum(c - 1, 0) * CHUNK, CHUNK), 2 * CHUNK)
        cur_off = jnp.where(first_chunk, 0, CHUNK)
        shifts = [jnp.where(wcol == wrow + (cur_off - (SSM_CONV - 1 - k)), 1.0, 0.0).astype(BF16)
                  for k in range(SSM_CONV - 1)]
        xs = conv(xs_ref, cwx_ref, cbx_ref, rows, win, shifts)
        bm = conv(bm_ref, cwb_ref, cbb_ref, rows, win, shifts)
        cm = conv(cm_ref, cwc_ref, cbc_ref, rows, win, shifts)
        dt_raw = dt_ref[rows, :] + dtb_ref[...]
        dt = jnp.maximum(dt_raw, 0.0) + jnp.log(1.0 + jnp.exp(-jnp.abs(dt_raw)))
        da = dt * (-jnp.exp(alog_ref[...]))
        cs_col = jnp.dot(lower, da, precision=hi, preferred_element_type=F32)
        cs_row = jnp.dot(da.T, upper, precision=hi, preferred_element_type=F32)
        cs_last = cs_col[CHUNK - 1:CHUNK, :]
        to_end = jnp.exp(cs_last - cs_col)
        from_start = jnp.exp(cs_col)
        end_decay = jnp.exp(cs_last)
        dsk = dsk_ref[...]
        for g in range(SSM_G):
            bg = bm[:, g * SSM_N:(g + 1) * SSM_N]
            cg = cm[:, g * SSM_N:(g + 1) * SSM_N].astype(BF16)
            cb = lax.dot_general(cg, bg.astype(BF16), (((1,), (1,)), ((), ())),
                                 preferred_element_type=F32)
            bg_t = bg.T.astype(BF16)
            ys, sq = [], []
            for pr in range(2):
                h0 = g * 4 + pr * 2
                blk = slice(h0 * SSM_P, (h0 + 2) * SSM_P)
                xh = xs[:, blk]
                xdt = xh * _pair(dt, h0, (CHUNK, LANES))
                wts = []
                for hh in (h0, h0 + 1):
                    seg = cs_col[:, hh:hh + 1] - cs_row[hh:hh + 1, :]
                    wts.append((cb * jnp.exp(jnp.where(causal, seg, -jnp.inf))).astype(BF16))
                x_stack = jnp.concatenate([jnp.where(first, xdt, 0.0), jnp.where(first, 0.0, xdt)],
                                          axis=0).astype(BF16)
                y_diag = jnp.dot(jnp.concatenate(wts, axis=1), x_stack, preferred_element_type=F32)
                state = st_ref[g, pr]
                y_off = jnp.dot(cg, state.astype(BF16), preferred_element_type=F32)
                y_off = y_off * _pair(from_start, h0, (CHUNK, LANES))
                upd = (xdt * _pair(to_end, h0, (CHUNK, LANES))).astype(BF16)
                st_ref[g, pr] = (state * _pair(end_decay, h0, (1, LANES))
                                 + jnp.dot(bg_t, upd, preferred_element_type=F32))
                y = y_diag + y_off + xh * _pair(dsk, h0, (1, LANES))
                y = y * _silu(z_ref[rows, blk].astype(F32))
                ys.append(y)
                sq.append(y * y)
            ssq = _row_sums(jnp.concatenate(sq, axis=1))
            inv = lax.rsqrt(ssq * (1.0 / (4 * SSM_P)) + EPS)
            for pr in range(2):
                h0 = g * 4 + pr * 2
                blk = slice(h0 * SSM_P, (h0 + 2) * SSM_P)
                o_ref[rows, blk] = (ys[pr] * inv * nw_ref[:, blk]).astype(BF16)
        return carry

    lax.fori_loop(0, n_chunks, step, 0, unroll=2)


def _ssd_call(proj, dt, conv_w, conv_b, dt_bias, a_log, d_skip, norm_w, *, batch, seq):
    t = proj.shape[0]
    gn = SSM_G * SSM_N
    pad = lambda v: jnp.pad(v, (0, LANES - SSM_HEADS)).reshape(1, LANES)
    conv_b2 = conv_b.reshape(1, -1)
    return pl.pallas_call(
        _ssd_body,
        grid=(batch,),
        in_specs=[pl.BlockSpec((seq, SSM_INNER), lambda b: (b, P_SZ // SSM_INNER)),
                  pl.BlockSpec((seq, SSM_INNER), lambda b: (b, P_SX // SSM_INNER)),
                  pl.BlockSpec((seq, gn), lambda b: (b, P_SB // gn)),
                  pl.BlockSpec((seq, gn), lambda b: (b, P_SC // gn)),
                  pl.BlockSpec((seq, LANES), lambda b: (b, 0)),
                  pl.BlockSpec((SSM_CONV, SSM_INNER), lambda b: (0, 0)),
                  pl.BlockSpec((SSM_CONV, gn), lambda b: (0, SSM_INNER // gn)),
                  pl.BlockSpec((SSM_CONV, gn), lambda b: (0, SSM_INNER // gn + 1)),
                  pl.BlockSpec((1, SSM_INNER), lambda b: (0, 0)),
                  pl.BlockSpec((1, gn), lambda b: (0, SSM_INNER // gn)),
                  pl.BlockSpec((1, gn), lambda b: (0, SSM_INNER // gn + 1)),
                  pl.BlockSpec((1, LANES), lambda b: (0, 0)),
                  pl.BlockSpec((1, LANES), lambda b: (0, 0)),
                  pl.BlockSpec((1, LANES), lambda b: (0, 0)),
                  pl.BlockSpec((1, SSM_INNER), lambda b: (0, 0))],
        out_specs=pl.BlockSpec((seq, SSM_INNER), lambda b: (b, 0)),
        out_shape=jax.ShapeDtypeStruct((t, SSM_INNER), BF16),
        scratch_shapes=[pltpu.VMEM((SSM_G, 2, SSM_N, LANES), F32)],
        compiler_params=_cparams(1),
        name="ssd",
    )(proj, proj, proj, proj, dt, conv_w, conv_w, conv_w, conv_b2, conv_b2, conv_b2,
      pad(dt_bias), pad(a_log), pad(d_skip), norm_w.reshape(1, -1))


def _att_proj_body(x_ref, nw_ref, sh_ref, sc_ref, w_ref, o_ref):
    h = _rms(x_ref[...]) * nw_ref[...]
    h = (h * (1.0 + sc_ref[0]) + sh_ref[0]).astype(BF16)
    o_ref[...] = jnp.dot(h, w_ref[...], preferred_element_type=F32)


def _att_proj_call(x2, nw, ada3, w_att, *, seq, tm):
    t, d = x2.shape
    n = w_att.shape[1]
    per_b = seq // tm
    return pl.pallas_call(
        _att_proj_body,
        grid=(t // tm,),
        in_specs=[pl.BlockSpec((tm, d), lambda i: (i, 0)),
                  pl.BlockSpec((1, d), lambda i: (0, 0)),
                  pl.BlockSpec((1, 1, d), lambda i: (i // per_b, 0, 0)),
                  pl.BlockSpec((1, 1, d), lambda i: (i // per_b, 0, 1)),
                  pl.BlockSpec((d, n), lambda i: (0, 0))],
        out_specs=pl.BlockSpec((tm, n), lambda i: (i, 0)),
        out_shape=jax.ShapeDtypeStruct((t, n), F32),
        compiler_params=_cparams(1),
        name="att_proj",
    )(x2, nw, ada3, ada3, w_att)


def _band_body(q0_ref, q1_ref, k0_ref, k1_ref, v0_ref, v1_ref, o_ref, lse_ref, *, dil):
    seq = q0_ref.shape[0]
    n_q = seq // dil // CHUNK
    qkv = ((q0_ref, k0_ref, v0_ref), (q1_ref, k1_ref, v1_ref))
    first = lax.broadcasted_iota(jnp.int32, (CHUNK, LANES), 1) < ATT_E

    def band_mask(nk):
        pos = lax.broadcasted_iota(jnp.int32, (2 * CHUNK, nk), 0) & (CHUNK - 1)
        dist = pos - lax.broadcasted_iota(jnp.int32, (2 * CHUNK, nk), 1) + (nk - CHUNK)
        return (dist >= 0) & (dist <= CHUNK)

    ok_cur, ok_both = band_mask(CHUNK), band_mask(2 * CHUNK)

    def rows(r, p0, n):
        return pl.ds(r + dil * p0, n, stride=dil) if dil > 1 else pl.ds(r + p0, n)

    for r in range(dil):
        for j in range(n_q):
            q_rows = rows(r, j * CHUNK, CHUNK)
            k_rows = rows(r, (j - 1) * CHUNK, 2 * CHUNK) if j else q_rows
            ok = ok_both if j else ok_cur
            for pr, (q_ref, k_ref, v_ref) in enumerate(qkv):
                q = q_ref[q_rows, :] * (ATT_E ** -0.5)
                qs = jnp.concatenate([jnp.where(first, q, 0.0), jnp.where(first, 0.0, q)],
                                     axis=0).astype(BF16)
                k = k_ref[k_rows, :].astype(BF16)
                v = v_ref[k_rows, :]
                s = lax.dot_general(qs, k, (((1,), (1,)), ((), ())), preferred_element_type=F32)
                s = jnp.where(ok, s, -jnp.inf)
                m = jnp.max(s, axis=-1, keepdims=True)
                p = jnp.exp(s - m)
                if j:
                    v = jnp.concatenate([v, jnp.ones_like(v)], axis=1)
                    pv = jnp.dot(p.astype(BF16), v.astype(BF16), preferred_element_type=F32)
                    l = pv[:, LANES:]
                    out = pv[:, :LANES] / l
                else:
                    l = jnp.sum(p, axis=-1, keepdims=True)
                    out = jnp.dot(p.astype(BF16), v.astype(BF16), preferred_element_type=F32) / l
                lse = m + jnp.log(l)
                o_ref[pr, q_rows, :] = jnp.where(first, out[:CHUNK], out[CHUNK:])
                lse_ref[pr, q_rows, :] = jnp.where(first, lse[:CHUNK], lse[CHUNK:])


def _band_call(att, gi, *, batch, seq):
    window, dil = ATT_PATTERNS[gi]
    assert window // dil == CHUNK and seq % (dil * CHUNK) == 0
    t = att.shape[0]
    pairs = ATT_SLOTS // 2
    col0 = gi * 3 * pairs
    in_specs = [pl.BlockSpec((seq, LANES), lambda b, c=col0 + part * pairs + pr: (b, c))
                for pr in range(pairs) for part in range(3)]
    order = [0, 3, 1, 4, 2, 5]
    out_spec = pl.BlockSpec((pairs, seq, LANES), lambda b: (0, b, 0))
    out_shape = jax.ShapeDtypeStruct((pairs, t, LANES), F32)
    return pl.pallas_call(
        functools.partial(_band_body, dil=dil),
        grid=(batch,),
        in_specs=[in_specs[i] for i in order],
        out_specs=[out_spec, out_spec],
        out_shape=[out_shape, out_shape],
        compiler_params=_cparams(1),
        name=f"band_attention_{gi}",
    )(*([att] * 6))


def _merge_body(x_ref, g0_ref, g1_ref, g2_ref, yr_ref, ys_ref, a0_ref, a1_ref, a2_ref,
                l0_ref, l1_ref, l2_ref, gate_ref, wr_ref, ws_ref, wa_ref, wo_ref, o_ref):
    att_proj = None
    for pr in range(ATT_SLOTS // 2):
        lses = [r[pr] for r in (l0_ref, l1_ref, l2_ref)]
        top = jnp.maximum(jnp.maximum(lses[0], lses[1]), lses[2])
        wts = [jnp.exp(l - top) for l in lses]
        num = sum(w * a[pr] for w, a in zip(wts, (a0_ref, a1_ref, a2_ref)))
        y_att = (num / (wts[0] + wts[1] + wts[2])).astype(BF16)
        part = jnp.dot(y_att, wa_ref[pr * LANES:(pr + 1) * LANES, :], preferred_element_type=F32)
        att_proj = part if att_proj is None else att_proj + part

    def branch(g_ref, y, w_ref):
        return jax.nn.sigmoid(g_ref[...].astype(F32)) * jnp.dot(
            y, w_ref[...], preferred_element_type=F32)

    mixed = (branch(g0_ref, yr_ref[...], wr_ref) + branch(g1_ref, ys_ref[...], ws_ref)
             + jax.nn.sigmoid(g2_ref[...].astype(F32)) * att_proj)
    upd = jnp.dot(mixed.astype(BF16), wo_ref[...], preferred_element_type=F32)
    o_ref[...] = x_ref[...] + gate_ref[0] * upd


def _merge_call(x2, proj, y_ret, y_ssm, att, ada3, wr, ws, wa, wo, *, seq, tm):
    t, d = x2.shape
    per_b = seq // tm
    const = lambda i: (0, 0)
    rows = lambda a: pl.BlockSpec((tm, a.shape[1]), lambda i: (i, 0))
    pair_rows = lambda a: pl.BlockSpec((a.shape[0], tm, a.shape[2]), lambda i: (0, i, 0))
    att_o = [o for o, _ in att]
    att_l = [l for _, l in att]
    return pl.pallas_call(
        _merge_body,
        grid=(t // tm,),
        in_specs=[pl.BlockSpec((tm, d), lambda i: (i, 0)),
                  pl.BlockSpec((tm, d), lambda i: (i, P_GATE // d)),
                  pl.BlockSpec((tm, d), lambda i: (i, P_GATE // d + 1)),
                  pl.BlockSpec((tm, d), lambda i: (i, P_GATE // d + 2)),
                  rows(y_ret), rows(y_ssm)]
                 + [pair_rows(a) for a in att_o] + [pair_rows(a) for a in att_l]
                 + [pl.BlockSpec((1, 1, d), lambda i: (i // per_b, 0, 2)),
                    pl.BlockSpec(wr.shape, const),
                    pl.BlockSpec(ws.shape, const),
                    pl.BlockSpec(wa.shape, const),
                    pl.BlockSpec(wo.shape, const)],
        out_specs=pl.BlockSpec((tm, d), lambda i: (i, 0)),
        out_shape=jax.ShapeDtypeStruct((t, d), F32),
        compiler_params=_cparams(1),
        name="merge_out",
    )(x2, proj, proj, proj, y_ret, y_ssm, *att_o, *att_l, ada3, wr, ws, wa, wo)


def _ffn_body(x_ref, nw_ref, sh_ref, sc_ref, gate_ref, w1_ref, w3_ref, w2_ref, fw_ref, o_ref,
              *, final_norm):
    x = x_ref[...]
    h = _rms(x) * nw_ref[...]
    h = (h * (1.0 + sc_ref[0]) + sh_ref[0]).astype(BF16)
    a = jnp.dot(h, w1_ref[...], preferred_element_type=F32)
    b = jnp.dot(h, w3_ref[...], preferred_element_type=F32)
    act = (_silu(a) * b).astype(BF16)
    y = x + gate_ref[0] * jnp.dot(act, w2_ref[...], preferred_element_type=F32)
    if final_norm:
        y = _rms(y) * fw_ref[...]
    o_ref[...] = y


def _ffn_call(x2, nw, ada3, w1, w3, w2, final_w, *, seq, tm, final_norm):
    t, d = x2.shape
    per_b = seq // tm
    const = lambda i: (0, 0)
    resident = dict(pipeline_mode=pl.Buffered(1))
    return pl.pallas_call(
        functools.partial(_ffn_body, final_norm=final_norm),
        grid=(t // tm,),
        in_specs=[pl.BlockSpec((tm, d), lambda i: (i, 0)),
                  pl.BlockSpec((1, d), const),
                  pl.BlockSpec((1, 1, d), lambda i: (i // per_b, 0, 3)),
                  pl.BlockSpec((1, 1, d), lambda i: (i // per_b, 0, 4)),
                  pl.BlockSpec((1, 1, d), lambda i: (i // per_b, 0, 5)),
                  pl.BlockSpec(w1.shape, const, **resident),
                  pl.BlockSpec(w3.shape, const, **resident),
                  pl.BlockSpec(w2.shape, const, **resident),
                  pl.BlockSpec((1, d), const)],
        out_specs=pl.BlockSpec((tm, d), lambda i: (i, 0)),
        out_shape=jax.ShapeDtypeStruct((t, d), F32),
        compiler_params=_cparams(1),
        name="ffn",
    )(x2, nw, ada3, ada3, ada3, w1, w3, w2, final_w)


def _tile(seq, want):
    return min(seq, want)


def kernel(x, c, ada_w, ada_b, norm1_w, w_in, ssm_conv_w, ssm_conv_b, ssm_dt_bias, ssm_a_log,
           ssm_d, ssm_norm_w, w_br_ret, w_br_ssm, w_br_att, w_out, norm2_w, w_ff1, w_ff3, w_ff2,
           final_norm_w):
    batch, seq, d = x.shape
    depth = ada_w.shape[0]
    assert d == D_MODEL and seq % (2 * CHUNK) == 0 and w_in.shape[2] == _O_END
    t = batch * seq
    ada = _ada_call(c, ada_w.astype(BF16), ada_b)
    tables = _ret_tables(seq)
    x2 = x.reshape(t, d)
    final_w = final_norm_w.reshape(1, d)
    for i in range(depth):
        ada3 = ada[i].reshape(batch, 1, 6 * d)
        w = w_in[i]
        w_main = jnp.concatenate([w[:, _O_GATE:_O_END], w[:, :_O_DT]], axis=1).astype(BF16)
        w_dt = jnp.pad(w[:, _O_DT:_O_AQ], ((0, 0), (0, LANES - SSM_HEADS))).astype(BF16)
        gw = ATT_SLOTS * ATT_E
        w_att = jnp.concatenate(
            [w[:, _O_AQ + part * ATT_W + g * gw:_O_AQ + part * ATT_W + (g + 1) * gw]
             for g in range(len(ATT_PATTERNS)) for part in range(3)], axis=1).astype(BF16)
        nw1 = norm1_w[i].reshape(1, d)
        proj, dt = _in_call(x2, nw1, ada3, w_main, w_dt, seq=seq, tm=_tile(seq, 1024), tn=3072)
        qkv = _att_proj_call(x2, nw1, ada3, w_att, seq=seq, tm=_tile(seq, 512))
        y_ret = _ret_call(proj, tables, batch=batch, seq=seq)
        y_ssm = _ssd_call(proj, dt, ssm_conv_w[i], ssm_conv_b[i], ssm_dt_bias[i], ssm_a_log[i],
                          ssm_d[i], ssm_norm_w[i], batch=batch, seq=seq)
        att = [_band_call(qkv, gi, batch=batch, seq=seq) for gi in range(len(ATT_PATTERNS))]
        x2 = _merge_call(x2, proj, y_ret, y_ssm, att, ada3, w_br_ret[i].astype(BF16),
                         w_br_ssm[i].astype(BF16), w_br_att[i].astype(BF16),
                         w_out[i].astype(BF16), seq=seq, tm=_tile(seq, 512))
        x2 = _ffn_call(x2, norm2_w[i].reshape(1, d), ada3, w_ff1[i].astype(BF16),
                       w_ff3[i].astype(BF16), w_ff2[i].astype(BF16), final_w,
                       seq=seq, tm=_tile(seq, 512), final_norm=(i == depth - 1))
    return x2.reshape(batch, seq, d)
```

```python
import functools
import math

import numpy as np
import jax
import jax.numpy as jnp
from jax import lax
from jax.experimental import pallas as pl
from jax.experimental.pallas import tpu as pltpu

F32 = jnp.float32
BF16 = jnp.bfloat16

D_MODEL = 1024
EPS = 1e-6
CHUNK = 128
RET_HEADS = 4
RET_QK = 128
RET_V = 256
ROPE_BASE = 10000.0
SSM_HEADS = 16
SSM_P = 64
SSM_G = 4
SSM_N = 128
SSM_CONV = 4
SSM_INNER = SSM_HEADS * SSM_P
ATT_PATTERNS = ((128, 1), (512, 4), (2048, 16))
ATT_SLOTS = 4
ATT_E = 64
ATT_W = len(ATT_PATTERNS) * ATT_SLOTS * ATT_E
N_BRANCHES = 3

_O_RQ = 0
_O_SXBC = 4096
_O_DT = 6144
_O_AQ = 6160
_O_GATE = 8464
_O_END = 11536
P_GATE = 0
P_RQ = 3072
P_RK = 3584
P_RV = 4096
P_RG = 5120
P_SZ = 6144
P_SX = 7168
P_SB = 8192
P_SC = 8704

LANES = 128
VMEM_LIMIT = 56 * 1024 * 1024


def _cparams(n_axes):
    return pltpu.CompilerParams(
        dimension_semantics=("arbitrary",) * n_axes, vmem_limit_bytes=VMEM_LIMIT)


def _silu(v):
    return v * jax.nn.sigmoid(v)


def _row_sums(v):
    hi = v.astype(BF16)
    lo = (v - hi.astype(F32)).astype(BF16)
    both = jnp.concatenate([hi, lo], axis=1)
    return jnp.dot(both, jnp.ones((both.shape[1], LANES), BF16), preferred_element_type=F32)


def _rms(v):
    return v * lax.rsqrt(jnp.mean(v * v, axis=-1, keepdims=True) + EPS)


def _ada_body(c_ref, w_ref, b_ref, o_ref):
    s = _silu(c_ref[...])
    o_ref[0] = jnp.dot(s.astype(BF16), w_ref[0], preferred_element_type=F32) + b_ref[0]


def _ada_call(c, ada_w, ada_b):
    depth, d, n = ada_w.shape
    b = c.shape[0]
    return pl.pallas_call(
        _ada_body,
        grid=(depth, n // d),
        in_specs=[pl.BlockSpec((b, d), lambda l, j: (0, 0)),
                  pl.BlockSpec((1, d, d), lambda l, j: (l, 0, j)),
                  pl.BlockSpec((1, 1, d), lambda l, j: (l, 0, j))],
        out_specs=pl.BlockSpec((1, b, d), lambda l, j: (l, 0, j)),
        out_shape=jax.ShapeDtypeStruct((depth, b, n), F32),
        compiler_params=_cparams(2),
        name="ada",
    )(c, ada_w, ada_b.reshape(depth, 1, n))


def _in_body(x_ref, nw_ref, sh_ref, sc_ref, w_ref, wdt_ref, o_ref, dt_ref, h_ref):
    @pl.when(pl.program_id(1) == 0)
    def _():
        h = _rms(x_ref[...]) * nw_ref[...]
        h = (h * (1.0 + sc_ref[0]) + sh_ref[0]).astype(BF16)
        h_ref[...] = h
        dt_ref[...] = jnp.dot(h, wdt_ref[...], preferred_element_type=F32)

    o_ref[...] = jnp.dot(h_ref[...], w_ref[...], preferred_element_type=F32).astype(BF16)


def _in_call(x2, nw, ada3, w_main, w_dt, *, layer, seq, tm, tn):
    t, d = x2.shape
    n = w_main.shape[2]
    per_b = seq // tm
    row0 = layer * (t // seq)
    return pl.pallas_call(
        _in_body,
        grid=(t // tm, n // tn),
        in_specs=[pl.BlockSpec((tm, d), lambda i, j: (i, 0)),
                  pl.BlockSpec((None, 1, d), lambda i, j: (layer, 0, 0)),
                  pl.BlockSpec((1, 1, d), lambda i, j: (row0 + i // per_b, 0, 0)),
                  pl.BlockSpec((1, 1, d), lambda i, j: (row0 + i // per_b, 0, 1)),
                  pl.BlockSpec((None, d, tn), lambda i, j: (layer, 0, j)),
                  pl.BlockSpec((None, d, LANES), lambda i, j: (layer, 0, 0))],
        out_specs=[pl.BlockSpec((tm, tn), lambda i, j: (i, j)),
                   pl.BlockSpec((tm, LANES), lambda i, j: (i, 0))],
        out_shape=[jax.ShapeDtypeStruct((t, n), BF16),
                   jax.ShapeDtypeStruct((t, LANES), F32)],
        scratch_shapes=[pltpu.VMEM((tm, d), BF16)],
        compiler_params=_cparams(2),
        name="in_proj",
    )(x2, nw, ada3, ada3, w_main, w_dt)


def _ret_tables(seq):
    half = RET_QK // 2
    inv_freq = ROPE_BASE ** (-np.arange(half, dtype=np.float64) / half)
    ang = np.arange(seq, dtype=np.float64)[:, None] * inv_freq[None, :]
    cos, sin = np.cos(ang), np.sin(ang)
    cos2 = np.concatenate([cos, cos], axis=1)
    sin2 = np.concatenate([-sin, sin], axis=1)
    lg = np.log1p(-np.exp(np.linspace(math.log(1.0 / 32), math.log(1.0 / 512), RET_HEADS)))
    idx = np.arange(CHUNK, dtype=np.float64)
    diff = idx[:, None] - idx[None, :]
    intra = np.where(diff[None] >= 0, np.exp(diff[None] * lg[:, None, None]), 0.0)
    cross_q = np.exp((idx[None, :, None] + 1.0) * lg[:, None, None])
    state_k = np.exp((CHUNK - 1.0 - idx[None, :, None]) * lg[:, None, None])
    chunk_decay = np.exp(CHUNK * lg)[:, None, None]
    as32 = lambda a: jnp.asarray(a, F32)
    return (as32(cos2), as32(sin2), as32(intra),
            as32(np.broadcast_to(cross_q, (RET_HEADS, CHUNK, RET_V))),
            as32(np.broadcast_to(state_k, (RET_HEADS, CHUNK, RET_QK))),
            as32(np.broadcast_to(chunk_decay, (RET_HEADS, 1, RET_V))))


def _ret_body(q_ref, k_ref, v_ref, g_ref, cos_ref, sin_ref, intra_ref, cq_ref, sk_ref, cd_ref,
              o_ref, r_ref):
    n_chunks = q_ref.shape[0] // CHUNK
    r_ref[...] = jnp.zeros_like(r_ref)

    def step(c, carry):
        rows = pl.ds(pl.multiple_of(c * CHUNK, CHUNK), CHUNK)
        cos = cos_ref[rows, :]
        sin = sin_ref[rows, :]
        for h in range(RET_HEADS):
            qk_cols = slice(h * RET_QK, (h + 1) * RET_QK)
            v_cols = slice(h * RET_V, (h + 1) * RET_V)
            q = q_ref[rows, qk_cols].astype(F32)
            k = k_ref[rows, qk_cols].astype(F32)
            qr = q * cos + pltpu.roll(q, RET_QK // 2, 1) * sin
            kr = (k * cos + pltpu.roll(k, RET_QK // 2, 1) * sin) * (RET_QK ** -0.5)
            qb = qr.astype(BF16)
            v = v_ref[rows, v_cols]
            scores = lax.dot_general(qb, kr.astype(BF16), (((1,), (1,)), ((), ())),
                                     preferred_element_type=F32) * intra_ref[h]
            inner = jnp.dot(scores.astype(BF16), v, preferred_element_type=F32)
            state = r_ref[h]
            cross = jnp.dot(qb, state.astype(BF16), preferred_element_type=F32) * cq_ref[h]
            ks_t = (kr * sk_ref[h]).T.astype(BF16)
            r_ref[h] = state * cd_ref[h] + jnp.dot(ks_t, v, preferred_element_type=F32)
            y = inner + cross
            dlt = y - jnp.mean(y, axis=-1, keepdims=True)
            yn = dlt * lax.rsqrt(jnp.mean(dlt * dlt, axis=-1, keepdims=True) + EPS)
            o_ref[rows, v_cols] = (_silu(g_ref[rows, v_cols].astype(F32)) * yn).astype(BF16)
        return carry

    lax.fori_loop(0, n_chunks, step, 0, unroll=2)


def _ret_call(proj, tables, *, batch, seq):
    cos2, sin2, intra, cq, sk, cd = tables
    t = proj.shape[0]
    wqk, wv = RET_HEADS * RET_QK, RET_HEADS * RET_V
    whole = lambda a: pl.BlockSpec(a.shape, lambda b: (0,) * a.ndim)
    return pl.pallas_call(
        _ret_body,
        grid=(batch,),
        in_specs=[pl.BlockSpec((seq, wqk), lambda b: (b, P_RQ // wqk)),
                  pl.BlockSpec((seq, wqk), lambda b: (b, P_RK // wqk)),
                  pl.BlockSpec((seq, wv), lambda b: (b, P_RV // wv)),
                  pl.BlockSpec((seq, wv), lambda b: (b, P_RG // wv)),
                  whole(cos2), whole(sin2), whole(intra), whole(cq), whole(sk), whole(cd)],
        out_specs=pl.BlockSpec((seq, wv), lambda b: (b, 0)),
        out_shape=jax.ShapeDtypeStruct((t, wv), BF16),
        scratch_shapes=[pltpu.VMEM((RET_HEADS, RET_QK, RET_V), F32)],
        compiler_params=_cparams(1),
        name="retention",
    )(proj, proj, proj, proj, cos2, sin2, intra, cq, sk, cd)


def _pair(col_src, h0, shape):
    lane = lax.broadcasted_iota(jnp.int32, shape, 1)
    if shape[0] == 1:
        a = jnp.broadcast_to(col_src[:, h0:h0 + 1], shape)
        b = jnp.broadcast_to(col_src[:, h0 + 1:h0 + 2], shape)
        return jnp.where(lane < SSM_P, a, b)
    return jnp.take_along_axis(col_src, h0 + (lane >> 6), axis=1)


def _ssd_body(z_ref, xs_ref, bm_ref, cm_ref, dt_ref, cwx_ref, cwb_ref, cwc_ref, cbx_ref, cbb_ref,
              cbc_ref, dtb_ref, alog_ref, dsk_ref, nw_ref, o_ref, st_ref):
    n_chunks = z_ref.shape[0] // CHUNK
    st_ref[...] = jnp.zeros_like(st_ref)
    row = lax.broadcasted_iota(jnp.int32, (CHUNK, CHUNK), 0)
    col = lax.broadcasted_iota(jnp.int32, (CHUNK, CHUNK), 1)
    causal = row >= col
    lower = causal.astype(F32)
    upper = (row <= col).astype(F32)
    lane = lax.broadcasted_iota(jnp.int32, (CHUNK, LANES), 1)
    first = lane < SSM_P
    hi = lax.Precision.HIGHEST

    wrow = lax.broadcasted_iota(jnp.int32, (CHUNK, 2 * CHUNK), 0)
    wcol = lax.broadcasted_iota(jnp.int32, (CHUNK, 2 * CHUNK), 1)

    def conv(x_ref, w_ref, b_ref, rows, win, shifts):
        window = x_ref[win, :]
        acc = b_ref[...] + x_ref[rows, :].astype(F32) * w_ref[SSM_CONV - 1:SSM_CONV, :]
        for k in range(SSM_CONV - 1):
            acc = acc + jnp.dot(shifts[k], window, preferred_element_type=F32) * w_ref[k:k + 1, :]
        return _silu(acc)

    def step(c, carry):
        rows = pl.ds(pl.multiple_of(c * CHUNK, CHUNK), CHUNK)
        first_chunk = c == 0
        win = pl.ds(pl.multiple_of(jnp.maximum(c - 1, 0) * CHUNK, CHUNK), 2 * CHUNK)
        cur_off = jnp.where(first_chunk, 0, CHUNK)
        shifts = [jnp.where(wcol == wrow + (cur_off - (SSM_CONV - 1 - k)), 1.0, 0.0).astype(BF16)
                  for k in range(SSM_CONV - 1)]
        xs = conv(xs_ref, cwx_ref, cbx_ref, rows, win, shifts)
        bm = conv(bm_ref, cwb_ref, cbb_ref, rows, win, shifts)
        cm = conv(cm_ref, cwc_ref, cbc_ref, rows, win, shifts)
        dt_raw = dt_ref[rows, :] + dtb_ref[...]
        dt = jnp.maximum(dt_raw, 0.0) + jnp.log(1.0 + jnp.exp(-jnp.abs(dt_raw)))
        da = dt * (-jnp.exp(alog_ref[...]))
        cs_col = jnp.dot(lower, da, precision=hi, preferred_element_type=F32)
        cs_row = jnp.dot(da.T, upper, precision=hi, preferred_element_type=F32)
        cs_last = cs_col[CHUNK - 1:CHUNK, :]
        to_end = jnp.exp(cs_last - cs_col)
        from_start = jnp.exp(cs_col)
        end_decay = jnp.exp(cs_last)
        dsk = dsk_ref[...]
        for g in range(SSM_G):
            bg = bm[:, g * SSM_N:(g + 1) * SSM_N]
            cg = cm[:, g * SSM_N:(g + 1) * SSM_N].astype(BF16)
            cb = lax.dot_general(cg, bg.astype(BF16), (((1,), (1,)), ((), ())),
                                 preferred_element_type=F32)
            bg_t = bg.T.astype(BF16)
            ys, sq = [], []
            for pr in range(2):
                h0 = g * 4 + pr * 2
                blk = slice(h0 * SSM_P, (h0 + 2) * SSM_P)
                xh = xs[:, blk]
                xdt = xh * _pair(dt, h0, (CHUNK, LANES))
                wts = []
                for hh in (h0, h0 + 1):
                    seg = cs_col[:, hh:hh + 1] - cs_row[hh:hh + 1, :]
                    wts.append((cb * jnp.exp(jnp.where(causal, seg, -jnp.inf))).astype(BF16))
                x_stack = jnp.concatenate([jnp.where(first, xdt, 0.0), jnp.where(first, 0.0, xdt)],
                                          axis=0).astype(BF16)
                y_diag = jnp.dot(jnp.concatenate(wts, axis=1), x_stack, preferred_element_type=F32)
                state = st_ref[g, pr]
                y_off = jnp.dot(cg, state.astype(BF16), preferred_element_type=F32)
                y_off = y_off * _pair(from_start, h0, (CHUNK, LANES))
                upd = (xdt * _pair(to_end, h0, (CHUNK, LANES))).astype(BF16)
                st_ref[g, pr] = (state * _pair(end_decay, h0, (1, LANES))
                                 + jnp.dot(bg_t, upd, preferred_element_type=F32))
                y = y_diag + y_off + xh * _pair(dsk, h0, (1, LANES))
                y = y * _silu(z_ref[rows, blk].astype(F32))
                ys.append(y)
                sq.append(y * y)
            ssq = _row_sums(jnp.concatenate(sq, axis=1))
            inv = lax.rsqrt(ssq * (1.0 / (4 * SSM_P)) + EPS)
            for pr in range(2):
                h0 = g * 4 + pr * 2
                blk = slice(h0 * SSM_P, (h0 + 2) * SSM_P)
                o_ref[rows, blk] = (ys[pr] * inv * nw_ref[:, blk]).astype(BF16)
        return carry

    lax.fori_loop(0, n_chunks, step, 0, unroll=2)


def _ssd_call(proj, dt, conv_w, conv_b, dt_bias, a_log, d_skip, norm_w, *, batch, seq):
    t = proj.shape[0]
    gn = SSM_G * SSM_N
    pad = lambda v: jnp.pad(v, (0, LANES - SSM_HEADS)).reshape(1, LANES)
    conv_b2 = conv_b.reshape(1, -1)
    return pl.pallas_call(
        _ssd_body,
        grid=(batch,),
        in_specs=[pl.BlockSpec((seq, SSM_INNER), lambda b: (b, P_SZ // SSM_INNER)),
                  pl.BlockSpec((seq, SSM_INNER), lambda b: (b, P_SX // SSM_INNER)),
                  pl.BlockSpec((seq, gn), lambda b: (b, P_SB // gn)),
                  pl.BlockSpec((seq, gn), lambda b: (b, P_SC // gn)),
                  pl.BlockSpec((seq, LANES), lambda b: (b, 0)),
                  pl.BlockSpec((SSM_CONV, SSM_INNER), lambda b: (0, 0)),
                  pl.BlockSpec((SSM_CONV, gn), lambda b: (0, SSM_INNER // gn)),
                  pl.BlockSpec((SSM_CONV, gn), lambda b: (0, SSM_INNER // gn + 1)),
                  pl.BlockSpec((1, SSM_INNER), lambda b: (0, 0)),
                  pl.BlockSpec((1, gn), lambda b: (0, SSM_INNER // gn)),
                  pl.BlockSpec((1, gn), lambda b: (0, SSM_INNER // gn + 1)),
                  pl.BlockSpec((1, LANES), lambda b: (0, 0)),
                  pl.BlockSpec((1, LANES), lambda b: (0, 0)),
                  pl.BlockSpec((1, LANES), lambda b: (0, 0)),
                  pl.BlockSpec((1, SSM_INNER), lambda b: (0, 0))],
        out_specs=pl.BlockSpec((seq, SSM_INNER), lambda b: (b, 0)),
        out_shape=jax.ShapeDtypeStruct((t, SSM_INNER), BF16),
        scratch_shapes=[pltpu.VMEM((SSM_G, 2, SSM_N, LANES), F32)],
        compiler_params=_cparams(1),
        name="ssd",
    )(proj, proj, proj, proj, dt, conv_w, conv_w, conv_w, conv_b2, conv_b2, conv_b2,
      pad(dt_bias), pad(a_log), pad(d_skip), norm_w.reshape(1, -1))


def _att_proj_body(x_ref, nw_ref, sh_ref, sc_ref, w_ref, o_ref):
    h = _rms(x_ref[...]) * nw_ref[...]
    h = (h * (1.0 + sc_ref[0]) + sh_ref[0]).astype(BF16)
    o_ref[...] = jnp.dot(h, w_ref[...], preferred_element_type=F32)


def _att_proj_call(x2, nw, ada3, w_att, *, layer, seq, tm):
    t, d = x2.shape
    n = w_att.shape[2]
    per_b = seq // tm
    row0 = layer * (t // seq)
    return pl.pallas_call(
        _att_proj_body,
        grid=(t // tm,),
        in_specs=[pl.BlockSpec((tm, d), lambda i: (i, 0)),
                  pl.BlockSpec((None, 1, d), lambda i: (layer, 0, 0)),
                  pl.BlockSpec((1, 1, d), lambda i: (row0 + i // per_b, 0, 0)),
                  pl.BlockSpec((1, 1, d), lambda i: (row0 + i // per_b, 0, 1)),
                  pl.BlockSpec((None, d, n), lambda i: (layer, 0, 0))],
        out_specs=pl.BlockSpec((tm, n), lambda i: (i, 0)),
        out_shape=jax.ShapeDtypeStruct((t, n), F32),
        compiler_params=_cparams(1),
        name="att_proj",
    )(x2, nw, ada3, ada3, w_att)


def _band_body(q0_ref, q1_ref, k0_ref, k1_ref, v0_ref, v1_ref, o_ref, lse_ref, *, dil):
    seq = q0_ref.shape[0]
    n_q = seq // dil // CHUNK
    qkv = ((q0_ref, k0_ref, v0_ref), (q1_ref, k1_ref, v1_ref))
    first = lax.broadcasted_iota(jnp.int32, (CHUNK, LANES), 1) < ATT_E

    def band_mask(nk):
        pos = lax.broadcasted_iota(jnp.int32, (2 * CHUNK, nk), 0) & (CHUNK - 1)
        dist = pos - lax.broadcasted_iota(jnp.int32, (2 * CHUNK, nk), 1) + (nk - CHUNK)
        return (dist >= 0) & (dist <= CHUNK)

    ok_cur, ok_both = band_mask(CHUNK), band_mask(2 * CHUNK)

    def rows(r, p0, n):
        return pl.ds(r + dil * p0, n, stride=dil) if dil > 1 else pl.ds(r + p0, n)

    for r in range(dil):
        for j in range(n_q):
            q_rows = rows(r, j * CHUNK, CHUNK)
            k_rows = rows(r, (j - 1) * CHUNK, 2 * CHUNK) if j else q_rows
            ok = ok_both if j else ok_cur
            for pr, (q_ref, k_ref, v_ref) in enumerate(qkv):
                q = q_ref[q_rows, :] * (ATT_E ** -0.5)
                qs = jnp.concatenate([jnp.where(first, q, 0.0), jnp.where(first, 0.0, q)],
                                     axis=0).astype(BF16)
                k = k_ref[k_rows, :].astype(BF16)
                v = v_ref[k_rows, :]
                s = lax.dot_general(qs, k, (((1,), (1,)), ((), ())), preferred_element_type=F32)
                s = jnp.where(ok, s, -jnp.inf)
                m = jnp.max(s, axis=-1, keepdims=True)
                p = jnp.exp(s - m)
                if j:
                    v = jnp.concatenate([v, jnp.ones_like(v)], axis=1)
                    pv = jnp.dot(p.astype(BF16), v.astype(BF16), preferred_element_type=F32)
                    l = pv[:, LANES:]
                    out = pv[:, :LANES] / l
                else:
                    l = jnp.sum(p, axis=-1, keepdims=True)
                    out = jnp.dot(p.astype(BF16), v.astype(BF16), preferred_element_type=F32) / l
                lse = m + jnp.log(l)
                o_ref[pr, q_rows, :] = jnp.where(first, out[:CHUNK], out[CHUNK:])
                lse_ref[pr, q_rows, :] = jnp.where(first, lse[:CHUNK], lse[CHUNK:])


def _band_call(att, gi, *, batch, seq):
    window, dil = ATT_PATTERNS[gi]
    assert window // dil == CHUNK and seq % (dil * CHUNK) == 0
    t = att.shape[0]
    pairs = ATT_SLOTS // 2
    col0 = gi * 3 * pairs
    in_specs = [pl.BlockSpec((seq, LANES), lambda b, c=col0 + part * pairs + pr: (b, c))
                for pr in range(pairs) for part in range(3)]
    order = [0, 3, 1, 4, 2, 5]
    out_spec = pl.BlockSpec((pairs, seq, LANES), lambda b: (0, b, 0))
    out_shape = jax.ShapeDtypeStruct((pairs, t, LANES), F32)
    return pl.pallas_call(
        functools.partial(_band_body, dil=dil),
        grid=(batch,),
        in_specs=[in_specs[i] for i in order],
        out_specs=[out_spec, out_spec],
        out_shape=[out_shape, out_shape],
        compiler_params=_cparams(1),
        name=f"band_attention_{gi}",
    )(*([att] * 6))


def _merge_body(x_ref, g0_ref, g1_ref, g2_ref, yr_ref, ys_ref, a0_ref, a1_ref, a2_ref,
                l0_ref, l1_ref, l2_ref, gate_ref, wr_ref, ws_ref, wa_ref, wo_ref, o_ref):
    att_proj = None
    for pr in range(ATT_SLOTS // 2):
        lses = [r[pr] for r in (l0_ref, l1_ref, l2_ref)]
        top = jnp.maximum(jnp.maximum(lses[0], lses[1]), lses[2])
        wts = [jnp.exp(l - top) for l in lses]
        num = sum(w * a[pr] for w, a in zip(wts, (a0_ref, a1_ref, a2_ref)))
        y_att = (num / (wts[0] + wts[1] + wts[2])).astype(BF16)
        part = jnp.dot(y_att, wa_ref[pr * LANES:(pr + 1) * LANES, :], preferred_element_type=F32)
        att_proj = part if att_proj is None else att_proj + part

    def branch(g_ref, y, w_ref):
        return jax.nn.sigmoid(g_ref[...].astype(F32)) * jnp.dot(
            y, w_ref[...], preferred_element_type=F32)

    mixed = (branch(g0_ref, yr_ref[...], wr_ref) + branch(g1_ref, ys_ref[...], ws_ref)
             + jax.nn.sigmoid(g2_ref[...].astype(F32)) * att_proj)
    upd = jnp.dot(mixed.astype(BF16), wo_ref[...], preferred_element_type=F32)
    o_ref[...] = x_ref[...] + gate_ref[0] * upd


def _merge_call(x2, proj, y_ret, y_ssm, att, ada3, wr, ws, wa, wo, *, layer, seq, tm):
    t, d = x2.shape
    per_b = seq // tm
    row0 = layer * (t // seq)
    layer_w = lambda w: pl.BlockSpec((None,) + w.shape[1:], lambda i: (layer, 0, 0))
    rows = lambda a: pl.BlockSpec((tm, a.shape[1]), lambda i: (i, 0))
    pair_rows = lambda a: pl.BlockSpec((a.shape[0], tm, a.shape[2]), lambda i: (0, i, 0))
    att_o = [o for o, _ in att]
    att_l = [l for _, l in att]
    return pl.pallas_call(
        _merge_body,
        grid=(t // tm,),
        in_specs=[pl.BlockSpec((tm, d), lambda i: (i, 0)),
                  pl.BlockSpec((tm, d), lambda i: (i, P_GATE // d)),
                  pl.BlockSpec((tm, d), lambda i: (i, P_GATE // d + 1)),
                  pl.BlockSpec((tm, d), lambda i: (i, P_GATE // d + 2)),
                  rows(y_ret), rows(y_ssm)]
                 + [pair_rows(a) for a in att_o] + [pair_rows(a) for a in att_l]
                 + [pl.BlockSpec((1, 1, d), lambda i: (row0 + i // per_b, 0, 2))]
                 + [layer_w(w) for w in (wr, ws, wa, wo)],
        out_specs=pl.BlockSpec((tm, d), lambda i: (i, 0)),
        out_shape=jax.ShapeDtypeStruct((t, d), F32),
        compiler_params=_cparams(1),
        name="merge_out",
    )(x2, proj, proj, proj, y_ret, y_ssm, *att_o, *att_l, ada3, wr, ws, wa, wo)


def _ffn_body(x_ref, nw_ref, sh_ref, sc_ref, gate_ref, w1_ref, w3_ref, w2_ref, fw_ref, o_ref,
              *, final_norm):
    x = x_ref[...]
    h = _rms(x) * nw_ref[...]
    h = (h * (1.0 + sc_ref[0]) + sh_ref[0]).astype(BF16)
    a = jnp.dot(h, w1_ref[...], preferred_element_type=F32)
    b = jnp.dot(h, w3_ref[...], preferred_element_type=F32)
    act = (_silu(a) * b).astype(BF16)
    y = x + gate_ref[0] * jnp.dot(act, w2_ref[...], preferred_element_type=F32)
    if final_norm:
        y = _rms(y) * fw_ref[...]
    o_ref[...] = y


def _ffn_call(x2, nw, ada3, w1, w3, w2, final_w, *, layer, seq, tm, final_norm):
    t, d = x2.shape
    per_b = seq // tm
    row0 = layer * (t // seq)
    layer_w = lambda w: pl.BlockSpec((None,) + w.shape[1:], lambda i: (layer, 0, 0),
                                     pipeline_mode=pl.Buffered(1))
    return pl.pallas_call(
        functools.partial(_ffn_body, final_norm=final_norm),
        grid=(t // tm,),
        in_specs=[pl.BlockSpec((tm, d), lambda i: (i, 0)),
                  pl.BlockSpec((None, 1, d), lambda i: (layer, 0, 0)),
                  pl.BlockSpec((1, 1, d), lambda i: (row0 + i // per_b, 0, 3)),
                  pl.BlockSpec((1, 1, d), lambda i: (row0 + i // per_b, 0, 4)),
                  pl.BlockSpec((1, 1, d), lambda i: (row0 + i // per_b, 0, 5)),
                  layer_w(w1), layer_w(w3), layer_w(w2),
                  pl.BlockSpec((1, d), lambda i: (0, 0))],
        out_specs=pl.BlockSpec((tm, d), lambda i: (i, 0)),
        out_shape=jax.ShapeDtypeStruct((t, d), F32),
        compiler_params=_cparams(1),
        name="ffn",
    )(x2, nw, ada3, ada3, ada3, w1, w3, w2, final_w)


def _tile(seq, want):
    return min(seq, want)


def kernel(x, c, ada_w, ada_b, norm1_w, w_in, ssm_conv_w, ssm_conv_b, ssm_dt_bias, ssm_a_log,
           ssm_d, ssm_norm_w, w_br_ret, w_br_ssm, w_br_att, w_out, norm2_w, w_ff1, w_ff3, w_ff2,
           final_norm_w):
    batch, seq, d = x.shape
    depth = ada_w.shape[0]
    assert d == D_MODEL and seq % (2 * CHUNK) == 0 and w_in.shape[2] == _O_END
    t = batch * seq
    ada = _ada_call(c, ada_w.astype(BF16), ada_b)
    tables = _ret_tables(seq)
    x2 = x.reshape(t, d)
    final_w = final_norm_w.reshape(1, d)
    ada3 = ada.reshape(depth * batch, 1, 6 * d)
    w_main = jnp.concatenate([w_in[:, :, _O_GATE:_O_END], w_in[:, :, :_O_DT]], axis=2).astype(BF16)
    w_dt = jnp.pad(w_in[:, :, _O_DT:_O_AQ], ((0, 0), (0, 0), (0, LANES - SSM_HEADS))).astype(BF16)
    gw = ATT_SLOTS * ATT_E
    w_att = jnp.concatenate(
        [w_in[:, :, _O_AQ + part * ATT_W + g * gw:_O_AQ + part * ATT_W + (g + 1) * gw]
         for g in range(len(ATT_PATTERNS)) for part in range(3)], axis=2).astype(BF16)
    nw1 = norm1_w.reshape(depth, 1, d)
    nw2 = norm2_w.reshape(depth, 1, d)
    branch_w = [w.astype(BF16) for w in (w_br_ret, w_br_ssm, w_br_att, w_out)]
    ffn_w = [w.astype(BF16) for w in (w_ff1, w_ff3, w_ff2)]
    for i in range(depth):
        proj, dt = _in_call(x2, nw1, ada3, w_main, w_dt, layer=i, seq=seq,
                            tm=_tile(seq, 1024), tn=3072)
        qkv = _att_proj_call(x2, nw1, ada3, w_att, layer=i, seq=seq, tm=_tile(seq, 512))
        y_ret = _ret_call(proj, tables, batch=batch, seq=seq)
        y_ssm = _ssd_call(proj, dt, ssm_conv_w[i], ssm_conv_b[i], ssm_dt_bias[i], ssm_a_log[i],
                          ssm_d[i], ssm_norm_w[i], batch=batch, seq=seq)
        att = [_band_call(qkv, gi, batch=batch, seq=seq) for gi in range(len(ATT_PATTERNS))]
        x2 = _merge_call(x2, proj, y_ret, y_ssm, att, ada3, *branch_w, layer=i, seq=seq,
                         tm=_tile(seq, 512))
        x2 = _ffn_call(x2, nw2, ada3, *ffn_w, final_w, layer=i, seq=seq, tm=_tile(seq, 512),
                       final_norm=(i == depth - 1))
    return x2.reshape(batch, seq, d)
```

```python
import functools
import math

import numpy as np
import jax
import jax.numpy as jnp
from jax import lax
from jax.experimental import pallas as pl
from jax.experimental.pallas import tpu as pltpu

F32 = jnp.float32
BF16 = jnp.bfloat16

D_MODEL = 1024
EPS = 1e-6
CHUNK = 128
RET_HEADS = 4
RET_QK = 128
RET_V = 256
ROPE_BASE = 10000.0
SSM_HEADS = 16
SSM_P = 64
SSM_G = 4
SSM_N = 128
SSM_CONV = 4
SSM_INNER = SSM_HEADS * SSM_P
ATT_PATTERNS = ((128, 1), (512, 4), (2048, 16))
ATT_SLOTS = 4
ATT_E = 64
ATT_W = len(ATT_PATTERNS) * ATT_SLOTS * ATT_E
N_BRANCHES = 3

_O_RQ = 0
_O_SXBC = 4096
_O_DT = 6144
_O_AQ = 6160
_O_GATE = 8464
_O_END = 11536
P_GATE = 0
P_RQ = 3072
P_RK = 3584
P_RV = 4096
P_RG = 5120
P_SZ = 6144
P_SX = 7168
P_SB = 8192
P_SC = 8704

LANES = 128
VMEM_LIMIT = 56 * 1024 * 1024


def _cparams(n_axes):
    return pltpu.CompilerParams(
        dimension_semantics=("arbitrary",) * n_axes, vmem_limit_bytes=VMEM_LIMIT)


def _silu(v):
    half = 0.5 * v
    return half * jnp.tanh(half) + half


def _row_sums(v):
    hi = v.astype(BF16)
    lo = (v - hi.astype(F32)).astype(BF16)
    both = jnp.concatenate([hi, lo], axis=1)
    return jnp.dot(both, jnp.ones((both.shape[1], LANES), BF16), preferred_element_type=F32)


def _rms(v):
    return v * lax.rsqrt(jnp.mean(v * v, axis=-1, keepdims=True) + EPS)


def _ada_body(c_ref, w_ref, b_ref, o_ref):
    s = _silu(c_ref[...])
    o_ref[0] = jnp.dot(s.astype(BF16), w_ref[0], preferred_element_type=F32) + b_ref[0]


def _ada_call(c, ada_w, ada_b):
    depth, d, n = ada_w.shape
    b = c.shape[0]
    return pl.pallas_call(
        _ada_body,
        grid=(depth, n // d),
        in_specs=[pl.BlockSpec((b, d), lambda l, j: (0, 0)),
                  pl.BlockSpec((1, d, d), lambda l, j: (l, 0, j)),
                  pl.BlockSpec((1, 1, d), lambda l, j: (l, 0, j))],
        out_specs=pl.BlockSpec((1, b, d), lambda l, j: (l, 0, j)),
        out_shape=jax.ShapeDtypeStruct((depth, b, n), F32),
        compiler_params=_cparams(2),
        name="ada",
    )(c, ada_w, ada_b.reshape(depth, 1, n))


def _in_body(x_ref, nw_ref, sh_ref, sc_ref, w_ref, wdt_ref, o_ref, dt_ref, h_ref):
    @pl.when(pl.program_id(1) == 0)
    def _():
        h = _rms(x_ref[...]) * nw_ref[...]
        h = (h * (1.0 + sc_ref[0]) + sh_ref[0]).astype(BF16)
        h_ref[...] = h
        dt_ref[...] = jnp.dot(h, wdt_ref[...], preferred_element_type=F32)

    o_ref[...] = jnp.dot(h_ref[...], w_ref[...], preferred_element_type=F32).astype(BF16)


def _in_call(x2, nw, ada3, w_main, w_dt, *, layer, seq, tm, tn):
    t, d = x2.shape
    n = w_main.shape[2]
    per_b = seq // tm
    row0 = layer * (t // seq)
    return pl.pallas_call(
        _in_body,
        grid=(t // tm, n // tn),
        in_specs=[pl.BlockSpec((tm, d), lambda i, j: (i, 0)),
                  pl.BlockSpec((None, 1, d), lambda i, j: (layer, 0, 0)),
                  pl.BlockSpec((1, 1, d), lambda i, j: (row0 + i // per_b, 0, 0)),
                  pl.BlockSpec((1, 1, d), lambda i, j: (row0 + i // per_b, 0, 1)),
                  pl.BlockSpec((None, d, tn), lambda i, j: (layer, 0, j)),
                  pl.BlockSpec((None, d, LANES), lambda i, j: (layer, 0, 0))],
        out_specs=[pl.BlockSpec((tm, tn), lambda i, j: (i, j)),
                   pl.BlockSpec((tm, LANES), lambda i, j: (i, 0))],
        out_shape=[jax.ShapeDtypeStruct((t, n), BF16),
                   jax.ShapeDtypeStruct((t, LANES), F32)],
        scratch_shapes=[pltpu.VMEM((tm, d), BF16)],
        compiler_params=_cparams(2),
        name="in_proj",
    )(x2, nw, ada3, ada3, w_main, w_dt)


def _ret_tables(seq):
    half = RET_QK // 2
    inv_freq = ROPE_BASE ** (-np.arange(half, dtype=np.float64) / half)
    ang = np.arange(seq, dtype=np.float64)[:, None] * inv_freq[None, :]
    cos, sin = np.cos(ang), np.sin(ang)
    cos2 = np.concatenate([cos, cos], axis=1)
    sin2 = np.concatenate([-sin, sin], axis=1)
    lg = np.log1p(-np.exp(np.linspace(math.log(1.0 / 32), math.log(1.0 / 512), RET_HEADS)))
    idx = np.arange(CHUNK, dtype=np.float64)
    diff = idx[:, None] - idx[None, :]
    intra = np.where(diff[None] >= 0, np.exp(diff[None] * lg[:, None, None]), 0.0)
    cross_q = np.exp((idx[None, :, None] + 1.0) * lg[:, None, None])
    state_k = np.exp((CHUNK - 1.0 - idx[None, :, None]) * lg[:, None, None])
    chunk_decay = np.exp(CHUNK * lg)[:, None, None]
    as32 = lambda a: jnp.asarray(a, F32)
    return (as32(cos2), as32(sin2), as32(intra),
            as32(np.broadcast_to(cross_q, (RET_HEADS, CHUNK, RET_V))),
            as32(np.broadcast_to(state_k, (RET_HEADS, CHUNK, RET_QK))),
            as32(np.broadcast_to(chunk_decay, (RET_HEADS, 1, RET_V))))


def _ret_body(q_ref, k_ref, v_ref, g_ref, cos_ref, sin_ref, intra_ref, cq_ref, sk_ref, cd_ref,
              o_ref, r_ref):
    n_chunks = q_ref.shape[0] // CHUNK
    r_ref[...] = jnp.zeros_like(r_ref)

    def step(c, carry):
        rows = pl.ds(pl.multiple_of(c * CHUNK, CHUNK), CHUNK)
        cos = cos_ref[rows, :]
        sin = sin_ref[rows, :]
        for h in range(RET_HEADS):
            qk_cols = slice(h * RET_QK, (h + 1) * RET_QK)
            v_cols = slice(h * RET_V, (h + 1) * RET_V)
            q = q_ref[rows, qk_cols].astype(F32)
            k = k_ref[rows, qk_cols].astype(F32)
            qr = q * cos + pltpu.roll(q, RET_QK // 2, 1) * sin
            kr = (k * cos + pltpu.roll(k, RET_QK // 2, 1) * sin) * (RET_QK ** -0.5)
            qb = qr.astype(BF16)
            v = v_ref[rows, v_cols]
            scores = lax.dot_general(qb, kr.astype(BF16), (((1,), (1,)), ((), ())),
                                     preferred_element_type=F32) * intra_ref[h]
            inner = jnp.dot(scores.astype(BF16), v, preferred_element_type=F32)
            state = r_ref[h]
            cross = jnp.dot(qb, state.astype(BF16), preferred_element_type=F32) * cq_ref[h]
            ks_t = (kr * sk_ref[h]).T.astype(BF16)
            r_ref[h] = state * cd_ref[h] + jnp.dot(ks_t, v, preferred_element_type=F32)
            y = inner + cross
            dlt = y - jnp.mean(y, axis=-1, keepdims=True)
            yn = dlt * lax.rsqrt(jnp.mean(dlt * dlt, axis=-1, keepdims=True) + EPS)
            o_ref[rows, v_cols] = (_silu(g_ref[rows, v_cols].astype(F32)) * yn).astype(BF16)
        return carry

    lax.fori_loop(0, n_chunks, step, 0, unroll=2)


def _ret_call(proj, tables, *, batch, seq):
    cos2, sin2, intra, cq, sk, cd = tables
    t = proj.shape[0]
    wqk, wv = RET_HEADS * RET_QK, RET_HEADS * RET_V
    whole = lambda a: pl.BlockSpec(a.shape, lambda b: (0,) * a.ndim)
    return pl.pallas_call(
        _ret_body,
        grid=(batch,),
        in_specs=[pl.BlockSpec((seq, wqk), lambda b: (b, P_RQ // wqk)),
                  pl.BlockSpec((seq, wqk), lambda b: (b, P_RK // wqk)),
                  pl.BlockSpec((seq, wv), lambda b: (b, P_RV // wv)),
                  pl.BlockSpec((seq, wv), lambda b: (b, P_RG // wv)),
                  whole(cos2), whole(sin2), whole(intra), whole(cq), whole(sk), whole(cd)],
        out_specs=pl.BlockSpec((seq, wv), lambda b: (b, 0)),
        out_shape=jax.ShapeDtypeStruct((t, wv), BF16),
        scratch_shapes=[pltpu.VMEM((RET_HEADS, RET_QK, RET_V), F32)],
        compiler_params=_cparams(1),
        name="retention",
    )(proj, proj, proj, proj, cos2, sin2, intra, cq, sk, cd)


def _pair(col_src, h0, shape):
    lane = lax.broadcasted_iota(jnp.int32, shape, 1)
    if shape[0] == 1:
        a = jnp.broadcast_to(col_src[:, h0:h0 + 1], shape)
        b = jnp.broadcast_to(col_src[:, h0 + 1:h0 + 2], shape)
        return jnp.where(lane < SSM_P, a, b)
    return jnp.take_along_axis(col_src, h0 + (lane >> 6), axis=1)


def _ssd_body(z_ref, xs_ref, bm_ref, cm_ref, dt_ref, cwx_ref, cwb_ref, cwc_ref, cbx_ref, cbb_ref,
              cbc_ref, dtb_ref, alog_ref, dsk_ref, nw_ref, o_ref, st_ref):
    n_chunks = z_ref.shape[0] // CHUNK
    st_ref[...] = jnp.zeros_like(st_ref)
    row = lax.broadcasted_iota(jnp.int32, (CHUNK, CHUNK), 0)
    col = lax.broadcasted_iota(jnp.int32, (CHUNK, CHUNK), 1)
    causal = row >= col
    lower = causal.astype(F32)
    upper = (row <= col).astype(F32)
    lane = lax.broadcasted_iota(jnp.int32, (CHUNK, LANES), 1)
    first = lane < SSM_P
    hi = lax.Precision.HIGHEST

    wrow = lax.broadcasted_iota(jnp.int32, (CHUNK, 2 * CHUNK), 0)
    wcol = lax.broadcasted_iota(jnp.int32, (CHUNK, 2 * CHUNK), 1)

    def conv(x_ref, w_ref, b_ref, rows, win, shifts):
        window = x_ref[win, :]
        acc = b_ref[...] + x_ref[rows, :].astype(F32) * w_ref[SSM_CONV - 1:SSM_CONV, :]
        for k in range(SSM_CONV - 1):
            acc = acc + jnp.dot(shifts[k], window, preferred_element_type=F32) * w_ref[k:k + 1, :]
        return _silu(acc)

    def step(c, carry):
        rows = pl.ds(pl.multiple_of(c * CHUNK, CHUNK), CHUNK)
        first_chunk = c == 0
        win = pl.ds(pl.multiple_of(jnp.maximum(c - 1, 0) * CHUNK, CHUNK), 2 * CHUNK)
        cur_off = jnp.where(first_chunk, 0, CHUNK)
        shifts = [jnp.where(wcol == wrow + (cur_off - (SSM_CONV - 1 - k)), 1.0, 0.0).astype(BF16)
                  for k in range(SSM_CONV - 1)]
        xs = conv(xs_ref, cwx_ref, cbx_ref, rows, win, shifts)
        bm = conv(bm_ref, cwb_ref, cbb_ref, rows, win, shifts)
        cm = conv(cm_ref, cwc_ref, cbc_ref, rows, win, shifts)
        dt_raw = dt_ref[rows, :] + dtb_ref[...]
        dt = jnp.maximum(dt_raw, 0.0) + jnp.log(1.0 + jnp.exp(-jnp.abs(dt_raw)))
        da = dt * (-jnp.exp(alog_ref[...]))
        cs_col = jnp.dot(lower, da, precision=hi, preferred_element_type=F32)
        cs_row = jnp.dot(da.T, upper, precision=hi, preferred_element_type=F32)
        cs_last = cs_col[CHUNK - 1:CHUNK, :]
        to_end = jnp.exp(cs_last - cs_col)
        from_start = jnp.exp(cs_col)
        end_decay = jnp.exp(cs_last)
        dsk = dsk_ref[...]
        for g in range(SSM_G):
            bg = bm[:, g * SSM_N:(g + 1) * SSM_N]
            cg = cm[:, g * SSM_N:(g + 1) * SSM_N].astype(BF16)
            cb = lax.dot_general(cg, bg.astype(BF16), (((1,), (1,)), ((), ())),
                                 preferred_element_type=F32)
            bg_t = bg.T.astype(BF16)
            ys, sq = [], []
            for pr in range(2):
                h0 = g * 4 + pr * 2
                blk = slice(h0 * SSM_P, (h0 + 2) * SSM_P)
                xh = xs[:, blk]
                xdt = xh * _pair(dt, h0, (CHUNK, LANES))
                wts = []
                for hh in (h0, h0 + 1):
                    seg = cs_col[:, hh:hh + 1] - cs_row[hh:hh + 1, :]
                    wts.append((cb * jnp.exp(jnp.where(causal, seg, -jnp.inf))).astype(BF16))
                x_stack = jnp.concatenate([jnp.where(first, xdt, 0.0), jnp.where(first, 0.0, xdt)],
                                          axis=0).astype(BF16)
                y_diag = jnp.dot(jnp.concatenate(wts, axis=1), x_stack, preferred_element_type=F32)
                state = st_ref[g, pr]
                y_off = jnp.dot(cg, state.astype(BF16), preferred_element_type=F32)
                y_off = y_off * _pair(from_start, h0, (CHUNK, LANES))
                upd = (xdt * _pair(to_end, h0, (CHUNK, LANES))).astype(BF16)
                st_ref[g, pr] = (state * _pair(end_decay, h0, (1, LANES))
                                 + jnp.dot(bg_t, upd, preferred_element_type=F32))
                y = y_diag + y_off + xh * _pair(dsk, h0, (1, LANES))
                y = y * _silu(z_ref[rows, blk].astype(F32))
                ys.append(y)
                sq.append(y * y)
            ssq = _row_sums(jnp.concatenate(sq, axis=1))
            inv = lax.rsqrt(ssq * (1.0 / (4 * SSM_P)) + EPS)
            for pr in range(2):
                h0 = g * 4 + pr * 2
                blk = slice(h0 * SSM_P, (h0 + 2) * SSM_P)
                o_ref[rows, blk] = (ys[pr] * inv * nw_ref[:, blk]).astype(BF16)
        return carry

    lax.fori_loop(0, n_chunks, step, 0, unroll=2)


def _ssd_call(proj, dt, conv_w, conv_b, dt_bias, a_log, d_skip, norm_w, *, batch, seq):
    t = proj.shape[0]
    gn = SSM_G * SSM_N
    pad = lambda v: jnp.pad(v, (0, LANES - SSM_HEADS)).reshape(1, LANES)
    conv_b2 = conv_b.reshape(1, -1)
    return pl.pallas_call(
        _ssd_body,
        grid=(batch,),
        in_specs=[pl.BlockSpec((seq, SSM_INNER), lambda b: (b, P_SZ // SSM_INNER)),
                  pl.BlockSpec((seq, SSM_INNER), lambda b: (b, P_SX // SSM_INNER)),
                  pl.BlockSpec((seq, gn), lambda b: (b, P_SB // gn)),
                  pl.BlockSpec((seq, gn), lambda b: (b, P_SC // gn)),
                  pl.BlockSpec((seq, LANES), lambda b: (b, 0)),
                  pl.BlockSpec((SSM_CONV, SSM_INNER), lambda b: (0, 0)),
                  pl.BlockSpec((SSM_CONV, gn), lambda b: (0, SSM_INNER // gn)),
                  pl.BlockSpec((SSM_CONV, gn), lambda b: (0, SSM_INNER // gn + 1)),
                  pl.BlockSpec((1, SSM_INNER), lambda b: (0, 0)),
                  pl.BlockSpec((1, gn), lambda b: (0, SSM_INNER // gn)),
                  pl.BlockSpec((1, gn), lambda b: (0, SSM_INNER // gn + 1)),
                  pl.BlockSpec((1, LANES), lambda b: (0, 0)),
                  pl.BlockSpec((1, LANES), lambda b: (0, 0)),
                  pl.BlockSpec((1, LANES), lambda b: (0, 0)),
                  pl.BlockSpec((1, SSM_INNER), lambda b: (0, 0))],
        out_specs=pl.BlockSpec((seq, SSM_INNER), lambda b: (b, 0)),
        out_shape=jax.ShapeDtypeStruct((t, SSM_INNER), BF16),
        scratch_shapes=[pltpu.VMEM((SSM_G, 2, SSM_N, LANES), F32)],
        compiler_params=_cparams(1),
        name="ssd",
    )(proj, proj, proj, proj, dt, conv_w, conv_w, conv_w, conv_b2, conv_b2, conv_b2,
      pad(dt_bias), pad(a_log), pad(d_skip), norm_w.reshape(1, -1))


def _att_proj_body(x_ref, nw_ref, sh_ref, sc_ref, w_ref, o_ref):
    h = _rms(x_ref[...]) * nw_ref[...]
    h = (h * (1.0 + sc_ref[0]) + sh_ref[0]).astype(BF16)
    o_ref[...] = jnp.dot(h, w_ref[...], preferred_element_type=F32)


def _att_proj_call(x2, nw, ada3, w_att, *, layer, seq, tm):
    t, d = x2.shape
    n = w_att.shape[2]
    per_b = seq // tm
    row0 = layer * (t // seq)
    return pl.pallas_call(
        _att_proj_body,
        grid=(t // tm,),
        in_specs=[pl.BlockSpec((tm, d), lambda i: (i, 0)),
                  pl.BlockSpec((None, 1, d), lambda i: (layer, 0, 0)),
                  pl.BlockSpec((1, 1, d), lambda i: (row0 + i // per_b, 0, 0)),
                  pl.BlockSpec((1, 1, d), lambda i: (row0 + i // per_b, 0, 1)),
                  pl.BlockSpec((None, d, n), lambda i: (layer, 0, 0))],
        out_specs=pl.BlockSpec((tm, n), lambda i: (i, 0)),
        out_shape=jax.ShapeDtypeStruct((t, n), F32),
        compiler_params=_cparams(1),
        name="att_proj",
    )(x2, nw, ada3, ada3, w_att)


def _band_body(q0_ref, q1_ref, k0_ref, k1_ref, v0_ref, v1_ref, o_ref, lse_ref, *, dil):
    seq = q0_ref.shape[0]
    n_q = seq // dil // CHUNK
    qkv = ((q0_ref, k0_ref, v0_ref), (q1_ref, k1_ref, v1_ref))
    first = lax.broadcasted_iota(jnp.int32, (CHUNK, LANES), 1) < ATT_E

    def band_mask(nk):
        pos = lax.broadcasted_iota(jnp.int32, (2 * CHUNK, nk), 0) & (CHUNK - 1)
        dist = pos - lax.broadcasted_iota(jnp.int32, (2 * CHUNK, nk), 1) + (nk - CHUNK)
        return (dist >= 0) & (dist <= CHUNK)

    ok_cur, ok_both = band_mask(CHUNK), band_mask(2 * CHUNK)

    def rows(r, p0, n):
        return pl.ds(r + dil * p0, n, stride=dil) if dil > 1 else pl.ds(r + p0, n)

    for r in range(dil):
        for j in range(n_q):
            q_rows = rows(r, j * CHUNK, CHUNK)
            k_rows = rows(r, (j - 1) * CHUNK, 2 * CHUNK) if j else q_rows
            ok = ok_both if j else ok_cur
            for pr, (q_ref, k_ref, v_ref) in enumerate(qkv):
                q = q_ref[q_rows, :] * (ATT_E ** -0.5)
                qs = jnp.concatenate([jnp.where(first, q, 0.0), jnp.where(first, 0.0, q)],
                                     axis=0).astype(BF16)
                k = k_ref[k_rows, :].astype(BF16)
                v = v_ref[k_rows, :]
                s = lax.dot_general(qs, k, (((1,), (1,)), ((), ())), preferred_element_type=F32)
                s = jnp.where(ok, s, -jnp.inf)
                m = jnp.max(s, axis=-1, keepdims=True)
                p = jnp.exp(s - m)
                if j:
                    v = jnp.concatenate([v, jnp.ones_like(v)], axis=1)
                    pv = jnp.dot(p.astype(BF16), v.astype(BF16), preferred_element_type=F32)
                    l = pv[:, LANES:]
                    out = pv[:, :LANES] / l
                else:
                    l = jnp.sum(p, axis=-1, keepdims=True)
                    out = jnp.dot(p.astype(BF16), v.astype(BF16), preferred_element_type=F32) / l
                lse = m + jnp.log(l)
                o_ref[pr, q_rows, :] = jnp.where(first, out[:CHUNK], out[CHUNK:])
                lse_ref[pr, q_rows, :] = jnp.where(first, lse[:CHUNK], lse[CHUNK:])


def _band_call(att, gi, *, batch, seq):
    window, dil = ATT_PATTERNS[gi]
    assert window // dil == CHUNK and seq % (dil * CHUNK) == 0
    t = att.shape[0]
    pairs = ATT_SLOTS // 2
    col0 = gi * 3 * pairs
    in_specs = [pl.BlockSpec((seq, LANES), lambda b, c=col0 + part * pairs + pr: (b, c))
                for pr in range(pairs) for part in range(3)]
    order = [0, 3, 1, 4, 2, 5]
    out_spec = pl.BlockSpec((pairs, seq, LANES), lambda b: (0, b, 0))
    out_shape = jax.ShapeDtypeStruct((pairs, t, LANES), F32)
    return pl.pallas_call(
        functools.partial(_band_body, dil=dil),
        grid=(batch,),
        in_specs=[in_specs[i] for i in order],
        out_specs=[out_spec, out_spec],
        out_shape=[out_shape, out_shape],
        compiler_params=_cparams(1),
        name=f"band_attention_{gi}",
    )(*([att] * 6))


def _merge_body(x_ref, g0_ref, g1_ref, g2_ref, yr_ref, ys_ref, a0_ref, a1_ref, a2_ref,
                l0_ref, l1_ref, l2_ref, gate_ref, wr_ref, ws_ref, wa_ref, wo_ref, o_ref):
    att_proj = None
    for pr in range(ATT_SLOTS // 2):
        lses = [r[pr] for r in (l0_ref, l1_ref, l2_ref)]
        top = jnp.maximum(jnp.maximum(lses[0], lses[1]), lses[2])
        wts = [jnp.exp(l - top) for l in lses]
        num = sum(w * a[pr] for w, a in zip(wts, (a0_ref, a1_ref, a2_ref)))
        y_att = (num / (wts[0] + wts[1] + wts[2])).astype(BF16)
        part = jnp.dot(y_att, wa_ref[pr * LANES:(pr + 1) * LANES, :], preferred_element_type=F32)
        att_proj = part if att_proj is None else att_proj + part

    def branch(g_ref, y, w_ref):
        return jax.nn.sigmoid(g_ref[...].astype(F32)) * jnp.dot(
            y, w_ref[...], preferred_element_type=F32)

    mixed = (branch(g0_ref, yr_ref[...], wr_ref) + branch(g1_ref, ys_ref[...], ws_ref)
             + jax.nn.sigmoid(g2_ref[...].astype(F32)) * att_proj)
    upd = jnp.dot(mixed.astype(BF16), wo_ref[...], preferred_element_type=F32)
    o_ref[...] = x_ref[...] + gate_ref[0] * upd


def _merge_call(x2, proj, y_ret, y_ssm, att, ada3, wr, ws, wa, wo, *, layer, seq, tm):
    t, d = x2.shape
    per_b = seq // tm
    row0 = layer * (t // seq)
    layer_w = lambda w: pl.BlockSpec((None,) + w.shape[1:], lambda i: (layer, 0, 0))
    rows = lambda a: pl.BlockSpec((tm, a.shape[1]), lambda i: (i, 0))
    pair_rows = lambda a: pl.BlockSpec((a.shape[0], tm, a.shape[2]), lambda i: (0, i, 0))
    att_o = [o for o, _ in att]
    att_l = [l for _, l in att]
    return pl.pallas_call(
        _merge_body,
        grid=(t // tm,),
        in_specs=[pl.BlockSpec((tm, d), lambda i: (i, 0)),
                  pl.BlockSpec((tm, d), lambda i: (i, P_GATE // d)),
                  pl.BlockSpec((tm, d), lambda i: (i, P_GATE // d + 1)),
                  pl.BlockSpec((tm, d), lambda i: (i, P_GATE // d + 2)),
                  rows(y_ret), rows(y_ssm)]
                 + [pair_rows(a) for a in att_o] + [pair_rows(a) for a in att_l]
                 + [pl.BlockSpec((1, 1, d), lambda i: (row0 + i // per_b, 0, 2))]
                 + [layer_w(w) for w in (wr, ws, wa, wo)],
        out_specs=pl.BlockSpec((tm, d), lambda i: (i, 0)),
        out_shape=jax.ShapeDtypeStruct((t, d), F32),
        compiler_params=_cparams(1),
        name="merge_out",
    )(x2, proj, proj, proj, y_ret, y_ssm, *att_o, *att_l, ada3, wr, ws, wa, wo)


def _ffn_body(x_ref, nw_ref, sh_ref, sc_ref, gate_ref, w1_ref, w3_ref, w2_ref, fw_ref, o_ref,
              *, final_norm):
    x = x_ref[...]
    h = _rms(x) * nw_ref[...]
    h = (h * (1.0 + sc_ref[0]) + sh_ref[0]).astype(BF16)
    a = jnp.dot(h, w1_ref[...], preferred_element_type=F32)
    b = jnp.dot(h, w3_ref[...], preferred_element_type=F32)
    act = (_silu(a) * b).astype(BF16)
    y = x + gate_ref[0] * jnp.dot(act, w2_ref[...], preferred_element_type=F32)
    if final_norm:
        y = _rms(y) * fw_ref[...]
    o_ref[...] = y


def _ffn_call(x2, nw, ada3, w1, w3, w2, final_w, *, layer, seq, tm, final_norm):
    t, d = x2.shape
    per_b = seq // tm
    row0 = layer * (t // seq)
    layer_w = lambda w: pl.BlockSpec((None,) + w.shape[1:], lambda i: (layer, 0, 0),
                                     pipeline_mode=pl.Buffered(1))
    return pl.pallas_call(
        functools.partial(_ffn_body, final_norm=final_norm),
        grid=(t // tm,),
        in_specs=[pl.BlockSpec((tm, d), lambda i: (i, 0)),
                  pl.BlockSpec((None, 1, d), lambda i: (layer, 0, 0)),
                  pl.BlockSpec((1, 1, d), lambda i: (row0 + i // per_b, 0, 3)),
                  pl.BlockSpec((1, 1, d), lambda i: (row0 + i // per_b, 0, 4)),
                  pl.BlockSpec((1, 1, d), lambda i: (row0 + i // per_b, 0, 5)),
                  layer_w(w1), layer_w(w3), layer_w(w2),
                  pl.BlockSpec((1, d), lambda i: (0, 0))],
        out_specs=pl.BlockSpec((tm, d), lambda i: (i, 0)),
        out_shape=jax.ShapeDtypeStruct((t, d), F32),
        compiler_params=_cparams(1),
        name="ffn",
    )(x2, nw, ada3, ada3, ada3, w1, w3, w2, final_w)


def _tile(seq, want):
    return min(seq, want)


def kernel(x, c, ada_w, ada_b, norm1_w, w_in, ssm_conv_w, ssm_conv_b, ssm_dt_bias, ssm_a_log,
           ssm_d, ssm_norm_w, w_br_ret, w_br_ssm, w_br_att, w_out, norm2_w, w_ff1, w_ff3, w_ff2,
           final_norm_w):
    batch, seq, d = x.shape
    depth = ada_w.shape[0]
    assert d == D_MODEL and seq % (2 * CHUNK) == 0 and w_in.shape[2] == _O_END
    t = batch * seq
    ada = _ada_call(c, ada_w.astype(BF16), ada_b)
    tables = _ret_tables(seq)
    x2 = x.reshape(t, d)
    final_w = final_norm_w.reshape(1, d)
    ada3 = ada.reshape(depth * batch, 1, 6 * d)
    w_main = jnp.concatenate([w_in[:, :, _O_GATE:_O_END], w_in[:, :, :_O_DT]], axis=2).astype(BF16)
    w_dt = jnp.pad(w_in[:, :, _O_DT:_O_AQ], ((0, 0), (0, 0), (0, LANES - SSM_HEADS))).astype(BF16)
    gw = ATT_SLOTS * ATT_E
    w_att = jnp.concatenate(
        [w_in[:, :, _O_AQ + part * ATT_W + g * gw:_O_AQ + part * ATT_W + (g + 1) * gw]
         for g in range(len(ATT_PATTERNS)) for part in range(3)], axis=2).astype(BF16)
    nw1 = norm1_w.reshape(depth, 1, d)
    nw2 = norm2_w.reshape(depth, 1, d)
    branch_w = [w.astype(BF16) for w in (w_br_ret, w_br_ssm, w_br_att, w_out)]
    ffn_w = [w.astype(BF16) for w in (w_ff1, w_ff3, w_ff2)]
    for i in range(depth):
        proj, dt = _in_call(x2, nw1, ada3, w_main, w_dt, layer=i, seq=seq,
                            tm=_tile(seq, 1024), tn=3072)
        qkv = _att_proj_call(x2, nw1, ada3, w_att, layer=i, seq=seq, tm=_tile(seq, 512))
        y_ret = _ret_call(proj, tables, batch=batch, seq=seq)
        y_ssm = _ssd_call(proj, dt, ssm_conv_w[i], ssm_conv_b[i], ssm_dt_bias[i], ssm_a_log[i],
                          ssm_d[i], ssm_norm_w[i], batch=batch, seq=seq)
        att = [_band_call(qkv, gi, batch=batch, seq=seq) for gi in range(len(ATT_PATTERNS))]
        x2 = _merge_call(x2, proj, y_ret, y_ssm, att, ada3, *branch_w, layer=i, seq=seq,
                         tm=_tile(seq, 512))
        x2 = _ffn_call(x2, nw2, ada3, *ffn_w, final_w, layer=i, seq=seq, tm=_tile(seq, 512),
                       final_norm=(i == depth - 1))
    return x2.reshape(batch, seq, d)
```
